```python
import functools
import jax, jax.numpy as jnp
from jax import lax
import numpy as np

D_MODEL = 4096
BATCH = 2
SEQ = 8192
DEPTH = 1
DEC_BATCH = 32
DEC_SEQ = 16
PAST_LEN = 1024

CHUNK = 64
Q_BLOCK = 128
EPS = 1e-6
ML_HEADS = 4
ML_WIDTH = D_MODEL // 2
ML_DV = ML_WIDTH // ML_HEADS
ML_DQK = ML_DV // 2
ML_QK = ML_HEADS * ML_DQK
MLA_HEADS = 16
MLA_NOPE = 128
MLA_ROPE = 64
MLA_DV = 128
MLA_WIDTH = MLA_HEADS * MLA_DV
Q_LORA = 1024
KV_LORA = 512
ROPE_THETA = 10000.0
MLA_SCALE = (MLA_NOPE + MLA_ROPE) ** -0.5
D_FF = 4 * D_MODEL
N_MOD = 6
IN_SIZES = (ML_QK, ML_QK, ML_WIDTH, ML_WIDTH, ML_HEADS, ML_HEADS, Q_LORA, KV_LORA, MLA_ROPE)
D_IN = sum(IN_SIZES)

kernel_name = 'hymba_mlstm_mla_streaming_step'


def rmsnorm(x, g):
    xf = x.astype(jnp.float32)
    y = xf * lax.rsqrt(jnp.mean(xf * xf, axis=-1, keepdims=True) + EPS)
    return (y * g.astype(jnp.float32)).astype(x.dtype)


def rope(x, pos):
    half = x.shape[-1] // 2
    inv = ROPE_THETA ** (-jnp.arange(half, dtype=jnp.float32) / half)
    ang = pos.astype(jnp.float32)[:, None] * inv[None, :]
    ang = ang.reshape((ang.shape[0],) + (1,) * (x.ndim - 3) + (half,))
    cos, sin = jnp.cos(ang), jnp.sin(ang)
    xf = x.astype(jnp.float32)
    x1, x2 = xf[..., :half], xf[..., half:]
    return jnp.concatenate([x1 * cos - x2 * sin, x2 * cos + x1 * sin], axis=-1).astype(x.dtype)


def modulation(c, w_ada, b_ada):
    mod = jnp.einsum('bd,de->be', jax.nn.silu(c), w_ada) + b_ada
    return jnp.split(mod[:, None, :], N_MOD, axis=-1)


def mixer_inputs(u, pos, w_in, b_ig, b_fg, g_cq, w_uq, g_ckv):
    bsz, s = u.shape[0], u.shape[1]
    z = jnp.einsum('bsd,de->bse', u, w_in)
    points, acc = [], 0
    for n in IN_SIZES[:-1]:
        acc += n
        points.append(acc)
    q, k, v, o, ig, fg, cq, ckv, kr = jnp.split(z, points, axis=-1)
    q = q.reshape(bsz, s, ML_HEADS, ML_DQK)
    k = k.reshape(bsz, s, ML_HEADS, ML_DQK) * (ML_DQK ** -0.5)
    v = v.reshape(bsz, s, ML_HEADS, ML_DV)
    ig = ig.astype(jnp.float32) + b_ig.astype(jnp.float32)
    lf = jax.nn.log_sigmoid(fg.astype(jnp.float32) + b_fg.astype(jnp.float32))
    qm = jnp.einsum('bsc,che->bshe', rmsnorm(cq, g_cq), w_uq)
    q_nope = qm[..., :MLA_NOPE]
    q_rope = rope(qm[..., MLA_NOPE:], pos)
    c_kv = rmsnorm(ckv, g_ckv)
    k_rope = rope(kr, pos)
    return q, k, v, o, ig, lf, q_nope, q_rope, c_kv, k_rope


def mlstm_chunk(state, q, k, v, ig, lf):
    C, n, m = state
    L = q.shape[2]
    bcum = jnp.cumsum(lf, axis=-1)
    a = bcum + m[..., None].astype(jnp.float32)
    dmat = bcum[..., :, None] - bcum[..., None, :] + ig[..., None, :]
    dmat = jnp.where(jnp.tril(jnp.ones((L, L), dtype=bool)), dmat, -jnp.inf)
    m_t = jnp.maximum(a, jnp.max(dmat, axis=-1))
    w_inter = jnp.exp(a - m_t)
    w_intra = jnp.exp(dmat - m_t[..., None])
    sc = jnp.einsum('bhtd,bhsd->bhts', q, k) * w_intra
    num = jnp.einsum('bhts,bhsv->bhtv', sc, v) + w_inter[..., None] * jnp.einsum('bhtd,bhdv->bhtv', q, C)
    nq = jnp.sum(sc, axis=-1) + w_inter * jnp.einsum('bhtd,bhd->bht', q, n)
    h = num / jnp.maximum(jnp.abs(nq), jnp.exp(-m_t))[..., None]
    m_new = m_t[..., -1]
    g_state = jnp.exp(bcum[..., -1] + m.astype(jnp.float32) - m_new)
    g_tok = jnp.exp(bcum[..., -1:] - bcum + ig - m_new[..., None])
    kg = k * g_tok[..., None]
    C_new = g_state[..., None, None] * C + jnp.einsum('bhsd,bhsv->bhdv', kg, v)
    n_new = g_state[..., None] * n + jnp.sum(kg, axis=2)
    return (C_new, n_new, m_new), h


def mlstm_prompt(q, k, v, ig, lf):
    bsz, s = q.shape[0], q.shape[1]
    nc = s // CHUNK

    def to_chunks(t):
        t = t.reshape((bsz, nc, CHUNK) + t.shape[2:])
        return jnp.swapaxes(jnp.moveaxis(t, 1, 0), 2, 3)

    init = (jnp.zeros((bsz, ML_HEADS, ML_DQK, ML_DV), jnp.float32),
            jnp.zeros((bsz, ML_HEADS, ML_DQK), jnp.float32),
            jnp.zeros((bsz, ML_HEADS), jnp.float32))
    xs = (to_chunks(q), to_chunks(k), to_chunks(v), to_chunks(ig), to_chunks(lf))
    state, h = lax.scan(lambda st, xc: mlstm_chunk(st, *xc), init, xs)
    h = jnp.moveaxis(jnp.swapaxes(h, 2, 3), 0, 1).reshape(bsz, s, ML_HEADS, ML_DV)
    return h, state


def mlstm_sample(q, k, v, ig, lf, C0, n0, m0):
    tr = lambda t: jnp.swapaxes(t, 1, 2)
    state, h = mlstm_chunk((C0, n0, m0), tr(q), tr(k), tr(v), tr(ig), tr(lf))
    return jnp.swapaxes(h, 1, 2), state


def mla_attend(q_nope, q_rope, c_kv, k_rope, w_uk, w_uv, mask):
    q_lat = jnp.einsum('bthd,chd->bthc', q_nope, w_uk)
    sc = jnp.einsum('bthc,bsc->bhts', q_lat, c_kv) + jnp.einsum('bthr,bsr->bhts', q_rope, k_rope)
    sc = sc.astype(jnp.float32) * MLA_SCALE
    if mask is not None:
        sc = jnp.where(mask, sc, -jnp.inf)
    p = jax.nn.softmax(sc, axis=-1).astype(c_kv.dtype)
    lat = jnp.einsum('bhts,bsc->bthc', p, c_kv)
    return jnp.einsum('bthc,chd->bthd', lat, w_uv)


def mla_prompt(q_nope, q_rope, c_kv, k_rope, w_uk, w_uv):
    bsz, s = q_nope.shape[0], q_nope.shape[1]
    nb = s // Q_BLOCK
    blocks = lambda t: jnp.moveaxis(t.reshape((bsz, nb, Q_BLOCK) + t.shape[2:]), 1, 0)
    k_chunk = jnp.arange(s) // CHUNK

    def one_block(args):
        i, qn, qr = args
        q_chunk = (i * Q_BLOCK + jnp.arange(Q_BLOCK)) // CHUNK
        mask = k_chunk[None, :] <= q_chunk[:, None]
        return mla_attend(qn, qr, c_kv, k_rope, w_uk, w_uv, mask)

    o = lax.map(one_block, (jnp.arange(nb), blocks(q_nope), blocks(q_rope)))
    return jnp.moveaxis(o, 0, 1).reshape(bsz, s, MLA_HEADS, MLA_DV)


def mla_sample(q_nope, q_rope, c_kv, k_rope, w_uk, w_uv, ckv_past, krope_past):
    c_all = jnp.concatenate([ckv_past.astype(c_kv.dtype), c_kv], axis=1)
    k_all = jnp.concatenate([krope_past.astype(k_rope.dtype), k_rope], axis=1)
    return mla_attend(q_nope, q_rope, c_all, k_all, w_uk, w_uv, None)


def trunk_layer(x, c, pos, run_mlstm, run_mla, w_ada, b_ada, g_pre1, g_post1, w_in, b_ig, b_fg,
                g_mlnorm, g_cq, w_uq, g_ckv, w_uk, w_uv, w_out, g_pre2, g_post2, w_ff1, w_ff2):
    bsz, s = x.shape[0], x.shape[1]
    sh1, sc1, gt1, sh2, sc2, gt2 = modulation(c, w_ada, b_ada)
    u = rmsnorm(x, g_pre1) * (1.0 + sc1) + sh1
    q, k, v, o, ig, lf, q_nope, q_rope, c_kv, k_rope = mixer_inputs(u, pos, w_in, b_ig, b_fg, g_cq, w_uq, g_ckv)
    h_ml, ml_state = run_mlstm(q, k, v, ig, lf)
    h_ml = rmsnorm(h_ml.astype(u.dtype), g_mlnorm.reshape(ML_HEADS, ML_DV))
    h_ml = h_ml.reshape(bsz, s, ML_WIDTH) * jax.nn.sigmoid(o)
    o_mla = run_mla(q_nope, q_rope, c_kv, k_rope, w_uk, w_uv).reshape(bsz, s, MLA_WIDTH)
    y = jnp.einsum('bse,ed->bsd', jnp.concatenate([h_ml, o_mla.astype(h_ml.dtype)], axis=-1), w_out)
    x = x + gt1 * rmsnorm(y, g_post1)
    u = rmsnorm(x, g_pre2) * (1.0 + sc2) + sh2
    f = jnp.einsum('bsf,fd->bsd', jnp.square(jax.nn.relu(jnp.einsum('bsd,df->bsf', u, w_ff1))), w_ff2)
    x = x + gt2 * rmsnorm(f, g_post2)
    return x, ml_state, c_kv, k_rope


def setup_inputs(seed: int = 0) -> dict:
    key = jax.random.key(seed)
    ks = jax.random.split(key, 32)
    f32 = jnp.float32
    L = DEPTH

    def nrm(k, shape, scale):
        return jax.random.normal(k, shape, f32) * scale

    def gain(k, n):
        return 1.0 + 0.02 * jax.random.normal(k, (L, n), f32)

    return {
        'x_prompt': nrm(ks[0], (BATCH, SEQ, D_MODEL), 1.0),
        'x_sample': nrm(ks[1], (DEC_BATCH, DEC_SEQ, D_MODEL), 1.0),
        'cache_mla_ckv': nrm(ks[2], (L, DEC_BATCH, PAST_LEN, KV_LORA), 1.0),
        'cache_mla_krope': nrm(ks[3], (L, DEC_BATCH, PAST_LEN, MLA_ROPE), 1.0),
        'state_mlstm_C': nrm(ks[4], (L, DEC_BATCH, ML_HEADS, ML_DQK, ML_DV), 0.05),
        'state_mlstm_n': jnp.abs(nrm(ks[5], (L, DEC_BATCH, ML_HEADS, ML_DQK), 0.1)),
        'state_mlstm_m': nrm(ks[6], (L, DEC_BATCH, ML_HEADS), 1.0),
        'c_prompt': nrm(ks[7], (BATCH, D_MODEL), 1.0),
        'c_sample': nrm(ks[8], (DEC_BATCH, D_MODEL), 1.0),
        'w_ada': nrm(ks[9], (L, D_MODEL, N_MOD * D_MODEL), D_MODEL ** -0.5),
        'b_ada': nrm(ks[10], (L, N_MOD * D_MODEL), 0.02),
        'g_pre1': gain(ks[11], D_MODEL),
        'g_post1': gain(ks[12], D_MODEL),
        'w_in': nrm(ks[13], (L, D_MODEL, D_IN), D_MODEL ** -0.5),
        'b_ig': nrm(ks[14], (L, ML_HEADS), 0.1),
        'b_fg': jnp.linspace(3.0, 6.0, ML_HEADS, dtype=f32)[None, :] + nrm(ks[15], (L, ML_HEADS), 0.1),
        'g_mlnorm': gain(ks[16], ML_WIDTH),
        'g_cq': gain(ks[17], Q_LORA),
        'w_uq': nrm(ks[18], (L, Q_LORA, MLA_HEADS, MLA_NOPE + MLA_ROPE), Q_LORA ** -0.5),
        'g_ckv': gain(ks[19], KV_LORA),
        'w_uk': nrm(ks[20], (L, KV_LORA, MLA_HEADS, MLA_NOPE), KV_LORA ** -0.5),
        'w_uv': nrm(ks[21], (L, KV_LORA, MLA_HEADS, MLA_DV), KV_LORA ** -0.5),
        'w_out': nrm(ks[22], (L, D_MODEL, D_MODEL), D_MODEL ** -0.5),
        'g_pre2': gain(ks[23], D_MODEL),
        'g_post2': gain(ks[24], D_MODEL),
        'w_ff1': nrm(ks[25], (L, D_MODEL, D_FF), D_MODEL ** -0.5),
        'w_ff2': nrm(ks[26], (L, D_FF, D_MODEL), D_FF ** -0.5),
    }


def reference(x_prompt, x_sample, cache_mla_ckv, cache_mla_krope, state_mlstm_C, state_mlstm_n,
              state_mlstm_m, c_prompt, c_sample, w_ada, b_ada, g_pre1, g_post1, w_in, b_ig, b_fg,
              g_mlnorm, g_cq, w_uq, g_ckv, w_uk, w_uv, w_out, g_pre2, g_post2, w_ff1, w_ff2):
    xp, xs = x_prompt, x_sample
    pos_p = jnp.arange(xp.shape[1])
    pos_s = PAST_LEN + jnp.arange(xs.shape[1])
    p_ckv, p_kr, p_C, p_n, p_m = [], [], [], [], []
    s_ckv, s_kr, s_C, s_n, s_m = [], [], [], [], []
    for l in range(DEPTH):
        lw = (w_ada[l], b_ada[l], g_pre1[l], g_post1[l], w_in[l], b_ig[l], b_fg[l], g_mlnorm[l],
              g_cq[l], w_uq[l], g_ckv[l], w_uk[l], w_uv[l], w_out[l], g_pre2[l], g_post2[l],
              w_ff1[l], w_ff2[l])
        xp, (cp, np_, mp), ckv_p, kr_p = trunk_layer(xp, c_prompt, pos_p, mlstm_prompt, mla_prompt, *lw)
        run_ml_s = functools.partial(mlstm_sample, C0=state_mlstm_C[l], n0=state_mlstm_n[l], m0=state_mlstm_m[l])
        run_mla_s = functools.partial(mla_sample, ckv_past=cache_mla_ckv[l], krope_past=cache_mla_krope[l])
        xs, (cs, ns, ms), ckv_s, kr_s = trunk_layer(xs, c_sample, pos_s, run_ml_s, run_mla_s, *lw)
        p_ckv.append(ckv_p); p_kr.append(kr_p); p_C.append(cp); p_n.append(np_); p_m.append(mp)
        s_ckv.append(ckv_s); s_kr.append(kr_s); s_C.append(cs); s_n.append(ns); s_m.append(ms)
    return (xp, xs, jnp.stack(p_ckv), jnp.stack(p_kr), jnp.stack(p_C), jnp.stack(p_n), jnp.stack(p_m),
            jnp.stack(s_ckv), jnp.stack(s_kr), jnp.stack(s_C), jnp.stack(s_n), jnp.stack(s_m))
```

```python
import functools
import math

import jax
import jax.numpy as jnp
from jax import lax
from jax.experimental import pallas as pl
from jax.experimental.pallas import tpu as pltpu

CHUNK = 64
EPS = 1e-6
ML_HEADS = 4
MLA_HEADS = 16
MLA_NOPE = 128
MLA_ROPE = 64
MLA_DV = 128
ROPE_THETA = 10000.0
N_MOD = 6
MLA_SCALE = (MLA_NOPE + MLA_ROPE) ** -0.5

LANES = 128
HEAD_PAD = 2 * LANES
SMALL_COLS = 2 * LANES
GATE_COL = LANES
VMEM_LIMIT_BYTES = 56 * 1024 * 1024
NEG = -1e30
GROUP_PAD = LANES
STRIP_ROWS = 128
F32 = jnp.float32
BF16 = jnp.bfloat16


def _cparams(*sem):
    return pltpu.CompilerParams(dimension_semantics=sem, vmem_limit_bytes=VMEM_LIMIT_BYTES)


def _dot(a, b):
    return jnp.dot(a, b, preferred_element_type=F32)


def _dot_nt(a, b):
    return lax.dot_general(a, b, (((1,), (1,)), ((), ())), preferred_element_type=F32)


def _dot_tn(a, b):
    return lax.dot_general(a, b, (((0,), (0,)), ((), ())), preferred_element_type=F32)


def _split3(a):
    hi = a.astype(BF16)
    r1 = a - hi.astype(F32)
    mid = r1.astype(BF16)
    lo = (r1 - mid.astype(F32)).astype(BF16)
    return hi, mid, lo


def _rms(x, g):
    return x * lax.rsqrt(jnp.mean(x * x, axis=-1, keepdims=True) + EPS) * g


def _accumulate_cols(o_ref, a, w_ref, first, width=512):
    @pl.when(first)
    def _():
        o_ref[...] = jnp.zeros(o_ref.shape, o_ref.dtype)

    n = o_ref.shape[1]
    width = min(width, n)
    for c0 in range(0, n, width):
        o_ref[:, c0:c0 + width] += _dot(a, w_ref[:, c0:c0 + width])


def _const_spec(shape):
    nd = len(shape)
    return pl.BlockSpec(shape, lambda *_: (0,) * nd, pipeline_mode=pl.Buffered(1))


def _mod_kernel(c_ref, w_ref, b_ref, o_ref):
    c = c_ref[...]
    a = (c * jax.nn.sigmoid(c)).astype(BF16)
    o_ref[...] = _dot(a, w_ref[...].astype(BF16)) + b_ref[...]


def _modulation(c_all, w_ada, b_ada):
    rows, d = c_all.shape
    n = w_ada.shape[1]
    tn = 512 if n % 512 == 0 else n
    return pl.pallas_call(
        _mod_kernel,
        grid=(n // tn,),
        in_specs=[pl.BlockSpec((rows, d), lambda j: (0, 0)),
                  pl.BlockSpec((d, tn), lambda j: (0, j)),
                  pl.BlockSpec((1, tn), lambda j: (0, j))],
        out_specs=pl.BlockSpec((rows, tn), lambda j: (0, j)),
        out_shape=jax.ShapeDtypeStruct((rows, n), F32),
        compiler_params=_cparams("arbitrary"),
        name="adaln_mod",
    )(c_all, w_ada, b_ada)


def _mod_operand(m, rows_per_group, tm):
    d = m.shape[-1]
    if rows_per_group >= tm:
        bpg = rows_per_group // tm
        return m[:, None, :], pl.BlockSpec((None, 1, d), lambda i, j: (i // bpg, 0, 0))
    gpb = tm // rows_per_group
    m3 = m.reshape(m.shape[0] // gpb, gpb, d)
    m3 = jnp.pad(m3, ((0, 0), (0, GROUP_PAD - gpb), (0, 0)))
    return m3, pl.BlockSpec((None, GROUP_PAD, d), lambda i, j: (i, 0, 0))


def _expand_mod(m_ref, r0, strip, rows_per_group):
    if m_ref.shape[0] == 1:
        return m_ref[...]
    groups = m_ref.shape[0]
    row = lax.broadcasted_iota(jnp.int32, (strip, groups), 0) + r0
    col = lax.broadcasted_iota(jnp.int32, (strip, groups), 1)
    sel = jnp.where(row // rows_per_group == col, 1.0, 0.0).astype(BF16)
    return sum(_dot(sel, part) for part in _split3(m_ref[...]))


def _for_strips(tm, body):
    strip = min(STRIP_ROWS, tm)

    def step(s, carry):
        body(pl.multiple_of(s * strip, strip), strip)
        return carry

    lax.fori_loop(0, tm // strip, step, 0)


def _modulated_norm(x_ref, g_ref, sc_ref, sh_ref, u_ref, rpg):
    def body(r0, strip):
        rows = pl.ds(r0, strip)
        xn = _rms(x_ref[rows, :], g_ref[...])
        u = xn * (1.0 + _expand_mod(sc_ref, r0, strip, rpg)) + _expand_mod(sh_ref, r0, strip, rpg)
        u_ref[rows, :] = u.astype(BF16)

    _for_strips(x_ref.shape[0], body)


def _gated_residual(o_ref, x_ref, gt_ref, g_ref, rpg):
    def body(r0, strip):
        rows = pl.ds(r0, strip)
        y = _rms(o_ref[rows, :], g_ref[...])
        o_ref[rows, :] = x_ref[rows, :] + _expand_mod(gt_ref, r0, strip, rpg) * y

    _for_strips(x_ref.shape[0], body)


def _inproj_kernel(x_ref, g_ref, sc_ref, sh_ref, w_ref, zm_ref, zs_ref, u_ref, *, n_main, rpg):
    j = pl.program_id(1)

    @pl.when(j == 0)
    def _():
        _modulated_norm(x_ref, g_ref, sc_ref, sh_ref, u_ref, rpg)

    r = _dot(u_ref[...], w_ref[...])

    @pl.when(j < n_main)
    def _():
        zm_ref[...] = r.astype(BF16)

    @pl.when(j == n_main)
    def _():
        zs_ref[...] = r[:, :SMALL_COLS]


def _inproj(x, g, sc, sh, w_packed, n_main_cols, tn, rows_per_group, tm):
    t, d = x.shape
    n_main = n_main_cols // tn
    sc_a, sc_s = _mod_operand(sc, rows_per_group, tm)
    sh_a, sh_s = _mod_operand(sh, rows_per_group, tm)
    return pl.pallas_call(
        functools.partial(_inproj_kernel, n_main=n_main, rpg=rows_per_group),
        grid=(t // tm, n_main + 1),
        in_specs=[pl.BlockSpec((tm, d), lambda i, j: (i, 0)),
                  pl.BlockSpec((1, d), lambda i, j: (0, 0)),
                  sc_s, sh_s,
                  pl.BlockSpec((d, tn), lambda i, j: (0, j))],
        out_specs=[pl.BlockSpec((tm, tn), lambda i, j: (i, jnp.minimum(j, n_main - 1))),
                   pl.BlockSpec((tm, SMALL_COLS), lambda i, j: (i, 0))],
        out_shape=[jax.ShapeDtypeStruct((t, n_main_cols), BF16),
                   jax.ShapeDtypeStruct((t, SMALL_COLS), F32)],
        scratch_shapes=[pltpu.VMEM((tm, d), BF16)],
        compiler_params=_cparams("arbitrary", "arbitrary"),
        name="in_proj",
    )(x, g, sc_a, sh_a, w_packed)


def _mlstm_kernel(bg_ref, q_ref, k_ref, v_ref, og_ref, zs_ref, gml_ref, c0_ref, n0_ref, m0_ref,
                  h_ref, c_ref, n_ref, m_ref, *, length, rows, scale):
    head = pl.program_id(1)
    chunk = pl.program_id(2)

    @pl.when(chunk == 0)
    def _():
        c_ref[...] = c0_ref[...]
        n_ref[...] = n0_ref[...]
        m_ref[...] = m0_ref[...]

    def pad(a):
        if rows == length:
            return a
        return jnp.concatenate([a, jnp.zeros((length - rows, a.shape[1]), a.dtype)], axis=0)

    q = pad(q_ref[...])
    k = pad(k_ref[...])
    v = pad(v_ref[...])
    zs = pad(zs_ref[...])

    srow = lax.broadcasted_iota(jnp.int32, (16, SMALL_COLS), 0)
    scol = lax.broadcasted_iota(jnp.int32, (16, SMALL_COLS), 1)
    want = jnp.where(srow < 8, GATE_COL + head, GATE_COL + ML_HEADS + head)
    sel = jnp.where(scol == want, 1.0, 0.0).astype(BF16)
    gates = sum(_dot_nt(sel, part) for part in _split3(zs))
    lane = lax.broadcasted_iota(jnp.int32, (1, length), 1)
    valid = lane < rows
    ig_row = jnp.where(valid, gates[0:1] + bg_ref[0, head], NEG)
    fpre = gates[8:9] + bg_ref[1, head]
    lf_row = jnp.where(valid, jnp.minimum(fpre, 0.0) - jnp.log(1.0 + jnp.exp(-jnp.abs(fpre))), 0.0)

    r_i = lax.broadcasted_iota(jnp.int32, (length, length), 0)
    c_i = lax.broadcasted_iota(jnp.int32, (length, length), 1)
    upper = jnp.where(r_i <= c_i, 1.0, 0.0).astype(BF16)
    lf8 = jnp.broadcast_to(lf_row, (8, length))
    bc_row = sum(_dot(part, upper) for part in _split3(lf8))[0:1]
    eye = jnp.where(r_i == c_i, 1.0, 0.0).astype(BF16)
    stacked = jnp.concatenate([jnp.broadcast_to(bc_row, (LANES // 2, length)),
                               jnp.broadcast_to(ig_row, (LANES // 2, length))], axis=0)
    cols = sum(_dot_nt(eye, part) for part in _split3(stacked))
    bc_col = cols[:, 0:1]
    ig_col = cols[:, LANES // 2:LANES // 2 + 1]

    m_prev = m_ref[...]
    dmat = jnp.where(r_i >= c_i, bc_col - bc_row + ig_row, NEG)
    a_col = bc_col + m_prev
    m_t = jnp.maximum(a_col, jnp.max(dmat, axis=1, keepdims=True))
    w_intra = jnp.exp(dmat - m_t)
    w_inter = jnp.exp(a_col - m_t)
    sc = _dot_nt(q, k) * scale * w_intra
    c_state = c_ref[...]
    n_state = n_ref[...]
    num = _dot(sc.astype(BF16), v) + w_inter * _dot(q, c_state.astype(BF16))
    qn = jnp.sum(q.astype(F32) * n_state, axis=1, keepdims=True)
    nq = jnp.sum(sc, axis=1, keepdims=True) + w_inter * qn
    hh = num / jnp.maximum(jnp.abs(nq), jnp.exp(-m_t))

    hv = hh[:rows]
    hn = _rms(hv, gml_ref[...])
    h_ref[...] = (hn * jax.nn.sigmoid(og_ref[...].astype(F32))).astype(BF16)

    m_new = m_t[length - 1:length, :]
    bc_last = bc_row[:, length - 1:length]
    g_state = jnp.exp(bc_last + m_prev - m_new)
    g_tok = jnp.exp(bc_last - bc_col + ig_col - m_new)
    kg = k.astype(F32) * (scale * g_tok)
    c_ref[...] = g_state * c_state + _dot_tn(kg.astype(BF16), v)
    n_ref[...] = g_state * n_state + jnp.sum(kg, axis=0, keepdims=True)
    m_ref[...] = m_new


def _mlstm(zm, zs, bg, g_mlnorm, c0, n0, m0, batch, seq, rows, length, dqk, dv):
    t = zm.shape[0]
    nc = seq // rows
    hh = ML_HEADS
    kernel = functools.partial(_mlstm_kernel, length=length, rows=rows, scale=dqk ** -0.5)
    row = lambda b, h, c: b * nc + c
    st4 = lambda b, h, c: (b, h, 0, 0)
    outs = pl.pallas_call(
        kernel,
        grid=(batch, hh, nc),
        in_specs=[pl.BlockSpec(memory_space=pltpu.SMEM),
                  pl.BlockSpec((rows, dqk), lambda b, h, c: (row(b, h, c), h)),
                  pl.BlockSpec((rows, dqk), lambda b, h, c: (row(b, h, c), hh + h)),
                  pl.BlockSpec((rows, dv), lambda b, h, c: (row(b, h, c), hh + h)),
                  pl.BlockSpec((rows, dv), lambda b, h, c: (row(b, h, c), 2 * hh + h)),
                  pl.BlockSpec((rows, SMALL_COLS), lambda b, h, c: (row(b, h, c), 0)),
                  pl.BlockSpec((1, dv), lambda b, h, c: (0, h)),
                  pl.BlockSpec((None, None, dqk, dv), st4),
                  pl.BlockSpec((None, None, 1, dqk), st4),
                  pl.BlockSpec((None, None, 1, 1), st4)],
        out_specs=[pl.BlockSpec((rows, dv), lambda b, h, c: (row(b, h, c), h)),
                   pl.BlockSpec((None, None, dqk, dv), st4),
                   pl.BlockSpec((None, None, 1, dqk), st4),
                   pl.BlockSpec((None, None, 1, 1), st4)],
        out_shape=[jax.ShapeDtypeStruct((t, hh * dv), BF16),
                   jax.ShapeDtypeStruct((batch, hh, dqk, dv), F32),
                   jax.ShapeDtypeStruct((batch, hh, 1, dqk), F32),
                   jax.ShapeDtypeStruct((batch, hh, 1, 1), F32)],
        compiler_params=_cparams("arbitrary", "arbitrary", "arbitrary"),
        name="mlstm",
    )(bg, zm, zm, zm, zm, zs, g_mlnorm, c0, n0[:, :, None, :], m0[:, :, None, None])
    h, c, n, m = outs
    return h, c, n[:, :, 0, :], m[:, :, 0, 0]


def _mla_prep_kernel(cq_ref, ckv_ref, zs_ref, pos_ref, invf_ref, gcq_ref, gckv_ref,
                     wuq_ref, wuk_ref, wuvt_ref,
                     q_ref, k_ref, vt_ref, ckv_out_ref, kr_out_ref):
    tm = cq_ref.shape[0]
    ang = pos_ref[...] * invf_ref[...]
    lane = lax.broadcasted_iota(jnp.int32, (tm, LANES), 1)
    half = MLA_ROPE // 2
    cosv = jnp.cos(ang)
    sinv = jnp.sin(ang)
    c_t = jnp.where(lane < MLA_ROPE, cosv, 0.0)
    s_lo = jnp.where(lane < half, -sinv, 0.0)
    s_hi = jnp.where((lane >= half) & (lane < MLA_ROPE), sinv, 0.0)

    def rope(x):
        return x * c_t + pltpu.roll(x, LANES - half, 1) * s_lo + pltpu.roll(x, half, 1) * s_hi

    cqn = _rms(cq_ref[...].astype(F32), gcq_ref[...]).astype(BF16)
    for h in range(MLA_HEADS):
        qh = _dot(cqn, wuq_ref[:, h * HEAD_PAD:(h + 1) * HEAD_PAD])
        q_ref[h, :, 0:LANES] = (qh[:, 0:LANES] * MLA_SCALE).astype(BF16)
        q_ref[h, :, LANES:HEAD_PAD] = (rope(qh[:, LANES:HEAD_PAD]) * MLA_SCALE).astype(BF16)

    ckvn = _rms(ckv_ref[...].astype(F32), gckv_ref[...])
    ckv_out_ref[...] = ckvn
    cb = ckvn.astype(BF16)
    kr = rope(zs_ref[:, 0:LANES])
    kr_out_ref[...] = kr[:, 0:MLA_ROPE]
    krb = kr.astype(BF16)
    kn = _dot(cb, wuk_ref[...])
    for h in range(MLA_HEADS):
        k_ref[h, :, 0:LANES] = kn[:, h * MLA_NOPE:(h + 1) * MLA_NOPE].astype(BF16)
        k_ref[h, :, LANES:HEAD_PAD] = krb
    vt = _dot_nt(wuvt_ref[...], cb)
    vt_ref[...] = vt.reshape(MLA_HEADS, MLA_DV, tm).astype(BF16)


def _mla_prep(zm, zs, pos, invf, g_cq, g_ckv, wuq, wuk, wuvt, cq_off, ckv_off, tm):
    t = zm.shape[0]
    ql = g_cq.shape[-1]
    kvl = g_ckv.shape[-1]
    hh = MLA_HEADS
    return pl.pallas_call(
        _mla_prep_kernel,
        grid=(t // tm,),
        in_specs=[pl.BlockSpec((tm, ql), lambda i: (i, cq_off // ql)),
                  pl.BlockSpec((tm, kvl), lambda i: (i, ckv_off // kvl)),
                  pl.BlockSpec((tm, SMALL_COLS), lambda i: (i, 0)),
                  pl.BlockSpec((tm, 1), lambda i: (i, 0)),
                  _const_spec((1, LANES)),
                  _const_spec((1, ql)),
                  _const_spec((1, kvl)),
                  _const_spec(wuq.shape),
                  _const_spec(wuk.shape),
                  _const_spec(wuvt.shape)],
        out_specs=[pl.BlockSpec((hh, tm, HEAD_PAD), lambda i: (0, i, 0)),
                   pl.BlockSpec((hh, tm, HEAD_PAD), lambda i: (0, i, 0)),
                   pl.BlockSpec((hh, None, MLA_DV, tm), lambda i: (0, i, 0, 0)),
                   pl.BlockSpec((tm, kvl), lambda i: (i, 0)),
                   pl.BlockSpec((tm, MLA_ROPE), lambda i: (i, 0))],
        out_shape=[jax.ShapeDtypeStruct((hh, t, HEAD_PAD), BF16),
                   jax.ShapeDtypeStruct((hh, t, HEAD_PAD), BF16),
                   jax.ShapeDtypeStruct((hh, t // tm, MLA_DV, tm), BF16),
                   jax.ShapeDtypeStruct((t, kvl), F32),
                   jax.ShapeDtypeStruct((t, MLA_ROPE), F32)],
        compiler_params=_cparams("arbitrary"),
        name="mla_prep",
    )(zm, zm, zs, pos, invf, g_cq, g_ckv, wuq, wuk, wuvt)


def _attn_kernel(q_ref, k_ref, vt_ref, o_ref, acc_ref, m_ref, l_ref, *, seq, tq):
    nq = seq // tq

    def q_body(qi, carry):
        q0 = pl.multiple_of(qi * tq, tq)
        q = q_ref[pl.ds(q0, tq), :]
        m_ref[...] = jnp.full(m_ref.shape, NEG, F32)
        l_ref[...] = jnp.zeros(l_ref.shape, F32)
        acc_ref[...] = jnp.zeros(acc_ref.shape, F32)

        def kv_step(ki, masked):
            k0 = pl.multiple_of(ki * tq, tq)
            kb = k_ref[pl.ds(k0, tq), :]
            s = _dot_nt(kb, q)
            if masked:
                kc = lax.broadcasted_iota(jnp.int32, s.shape, 0) // CHUNK
                qc = lax.broadcasted_iota(jnp.int32, s.shape, 1) // CHUNK
                s = jnp.where(kc <= qc, s, NEG)
            m_prev = m_ref[...]
            m_new = jnp.maximum(m_prev, jnp.max(s, axis=0, keepdims=True))
            alpha = jnp.exp(m_prev - m_new)
            p = jnp.exp(s - m_new)
            l_ref[...] = alpha * l_ref[...] + jnp.sum(p, axis=0, keepdims=True)
            acc_ref[...] = alpha * acc_ref[...] + _dot(vt_ref[ki], p.astype(BF16))
            m_ref[...] = m_new

        def loop_body(ki, c):
            kv_step(ki, False)
            return c

        lax.fori_loop(0, qi, loop_body, 0)
        kv_step(qi, True)
        out = acc_ref[...] * (1.0 / l_ref[...])
        o_ref[pl.ds(q0, tq), :] = out.T.astype(BF16)
        return carry

    lax.fori_loop(0, nq, q_body, 0)


def _attention(q, k, vt, batch, seq, tq):
    hh = MLA_HEADS
    nblk = seq // tq
    return pl.pallas_call(
        functools.partial(_attn_kernel, seq=seq, tq=tq),
        grid=(batch, hh),
        in_specs=[pl.BlockSpec((None, seq, HEAD_PAD), lambda b, h: (h, b, 0)),
                  pl.BlockSpec((None, seq, HEAD_PAD), lambda b, h: (h, b, 0)),
                  pl.BlockSpec((None, nblk, MLA_DV, tq), lambda b, h: (h, b, 0, 0))],
        out_specs=pl.BlockSpec((seq, MLA_DV), lambda b, h: (b, h)),
        out_shape=jax.ShapeDtypeStruct((batch * seq, hh * MLA_DV), BF16),
        scratch_shapes=[pltpu.VMEM((MLA_DV, tq), F32),
                        pltpu.VMEM((1, tq), F32),
                        pltpu.VMEM((1, tq), F32)],
        compiler_params=_cparams("arbitrary", "arbitrary"),
        name="mla_attention",
    )(q, k, vt)


def _absorb_kernel(q_ref, wukt_ref, o_ref, *, kvl):
    q = q_ref[...]
    ql = _dot(q[:, 0:MLA_NOPE], wukt_ref[...])
    full = jnp.concatenate([ql.astype(BF16), q[:, LANES:HEAD_PAD]], axis=1)
    o_ref[...] = full.reshape(o_ref.shape)


def _sample_attn_kernel(q_ref, cpast_ref, kpast_ref, cnew_ref, knew_ref, o_ref, *, kvl, new):
    qb = q_ref[...]
    ql = qb[:, 0:kvl]
    qr = qb[:, kvl:kvl + LANES]
    c_past = cpast_ref[...].astype(BF16)
    k_past = kpast_ref[...].astype(BF16)
    s_past = _dot_nt(ql, c_past) + _dot_nt(qr, k_past)
    fill = LANES - new
    c_new = jnp.concatenate([cnew_ref[...].astype(BF16), jnp.zeros((fill, kvl), BF16)], axis=0)
    k_new = jnp.concatenate([knew_ref[...], jnp.zeros((fill, LANES), BF16)], axis=0)
    s_new = _dot_nt(ql, c_new) + _dot_nt(qr, k_new)
    lane = lax.broadcasted_iota(jnp.int32, s_new.shape, 1)
    s_new = jnp.where(lane < new, s_new, NEG)
    m = jnp.maximum(jnp.max(s_past, axis=1, keepdims=True), jnp.max(s_new, axis=1, keepdims=True))
    p_past = jnp.exp(s_past - m)
    p_new = jnp.exp(s_new - m)
    l = jnp.sum(p_past, axis=1, keepdims=True) + jnp.sum(p_new, axis=1, keepdims=True)
    lat = _dot(p_past.astype(BF16), c_past) + _dot(p_new.astype(BF16), c_new)
    o_ref[...] = (lat / l).astype(BF16)


def _uv_kernel(lat_ref, wuv_ref, o_ref):
    lat = lat_ref[...]
    lat2 = lat.reshape(lat.shape[0] * lat.shape[1], lat.shape[2])
    o_ref[...] = _dot(lat2, wuv_ref[...]).astype(BF16)


def _sample_mla(q_s, k_s, ckv_new, cache_ckv, cache_krope_pad, wukt, wuv_h, batch, new):
    hh = MLA_HEADS
    t = batch * new
    kvl = ckv_new.shape[-1]
    width = kvl + LANES
    past = cache_ckv.shape[1]
    q_abs = pl.pallas_call(
        functools.partial(_absorb_kernel, kvl=kvl),
        grid=(hh,),
        in_specs=[pl.BlockSpec((None, t, HEAD_PAD), lambda h: (h, 0, 0)),
                  pl.BlockSpec((None, MLA_NOPE, kvl), lambda h: (h, 0, 0))],
        out_specs=pl.BlockSpec((batch, new, width), lambda h: (0, h, 0)),
        out_shape=jax.ShapeDtypeStruct((batch, hh * new, width), BF16),
        compiler_params=_cparams("arbitrary"),
        name="mla_absorb",
    )(q_s, wukt)
    lat = pl.pallas_call(
        functools.partial(_sample_attn_kernel, kvl=kvl, new=new),
        grid=(batch,),
        in_specs=[pl.BlockSpec((None, hh * new, width), lambda b: (b, 0, 0)),
                  pl.BlockSpec((None, past, kvl), lambda b: (b, 0, 0)),
                  pl.BlockSpec((None, past, LANES), lambda b: (b, 0, 0)),
                  pl.BlockSpec((new, kvl), lambda b: (b, 0)),
                  pl.BlockSpec((None, new, LANES), lambda b: (0, b, 1))],
        out_specs=pl.BlockSpec((None, hh * new, kvl), lambda b: (b, 0, 0)),
        out_shape=jax.ShapeDtypeStruct((batch, hh * new, kvl), BF16),
        compiler_params=_cparams("arbitrary"),
        name="mla_sample_attention",
    )(q_abs, cache_ckv, cache_krope_pad, ckv_new, k_s)
    return pl.pallas_call(
        _uv_kernel,
        grid=(hh,),
        in_specs=[pl.BlockSpec((batch, new, kvl), lambda h: (0, h, 0)),
                  pl.BlockSpec((None, kvl, MLA_DV), lambda h: (h, 0, 0))],
        out_specs=pl.BlockSpec((t, MLA_DV), lambda h: (0, h)),
        out_shape=jax.ShapeDtypeStruct((t, hh * MLA_DV), BF16),
        compiler_params=_cparams("arbitrary"),
        name="mla_uv",
    )(lat, wuv_h)


def _outproj_kernel(hm_ref, om_ref, w_ref, x_ref, gt_ref, g_ref, o_ref, *, half, rpg):
    kk = pl.program_id(1)
    a = jnp.where(kk < half, hm_ref[...], om_ref[...])
    _accumulate_cols(o_ref, a, w_ref, kk == 0)

    @pl.when(kk == 2 * half - 1)
    def _():
        _gated_residual(o_ref, x_ref, gt_ref, g_ref, rpg)


def _outproj(hml, omla, w_out, x, gt, g, rows_per_group, tm, tk):
    t, d = x.shape
    half = hml.shape[1] // tk
    gt_a, gt_s = _mod_operand(gt, rows_per_group, tm)
    return pl.pallas_call(
        functools.partial(_outproj_kernel, half=half, rpg=rows_per_group),
        grid=(t // tm, 2 * half),
        in_specs=[pl.BlockSpec((tm, tk), lambda i, k: (i, jnp.minimum(k, half - 1))),
                  pl.BlockSpec((tm, tk), lambda i, k: (i, jnp.maximum(k - half, 0))),
                  pl.BlockSpec((tk, d), lambda i, k: (k, 0)),
                  pl.BlockSpec((tm, d), lambda i, k: (i, 0)),
                  gt_s,
                  pl.BlockSpec((1, d), lambda i, k: (0, 0))],
        out_specs=pl.BlockSpec((tm, d), lambda i, k: (i, 0)),
        out_shape=jax.ShapeDtypeStruct((t, d), F32),
        compiler_params=_cparams("arbitrary", "arbitrary"),
        name="out_proj",
    )(hml, omla, w_out, x, gt_a, g)


def _ffn_kernel(x_ref, gpre_ref, sc_ref, sh_ref, gt_ref, gpost_ref, w1_ref, w2_ref, o_ref, u_ref,
                *, rpg):
    f = pl.program_id(1)

    @pl.when(f == 0)
    def _():
        _modulated_norm(x_ref, gpre_ref, sc_ref, sh_ref, u_ref, rpg)

    h = jnp.maximum(_dot(u_ref[...], w1_ref[...]), 0.0)
    _accumulate_cols(o_ref, (h * h).astype(BF16), w2_ref, f == 0)

    @pl.when(f == pl.num_programs(1) - 1)
    def _():
        _gated_residual(o_ref, x_ref, gt_ref, gpost_ref, rpg)


def _ffn(x, gpre, sc, sh, gt, gpost, w1, w2, rows_per_group, tm, tf):
    t, d = x.shape
    ff = w1.shape[1]
    sc_a, sc_s = _mod_operand(sc, rows_per_group, tm)
    sh_a, sh_s = _mod_operand(sh, rows_per_group, tm)
    gt_a, gt_s = _mod_operand(gt, rows_per_group, tm)
    vec = pl.BlockSpec((1, d), lambda i, f: (0, 0))
    return pl.pallas_call(
        functools.partial(_ffn_kernel, rpg=rows_per_group),
        grid=(t // tm, ff // tf),
        in_specs=[pl.BlockSpec((tm, d), lambda i, f: (i, 0)), vec, sc_s, sh_s, gt_s, vec,
                  pl.BlockSpec((d, tf), lambda i, f: (0, f)),
                  pl.BlockSpec((tf, d), lambda i, f: (f, 0))],
        out_specs=pl.BlockSpec((tm, d), lambda i, f: (i, 0)),
        out_shape=jax.ShapeDtypeStruct((t, d), F32),
        scratch_shapes=[pltpu.VMEM((tm, d), BF16)],
        compiler_params=_cparams("arbitrary", "arbitrary"),
        name="ffn",
    )(x, gpre, sc_a, sh_a, gt_a, gpost, w1, w2)


def _pack_weights(w_in, w_uq, w_uk, w_uv, w_out, w_ff1, w_ff2, d):
    ml_width = d // 2
    ml_dv = ml_width // ML_HEADS
    ml_dqk = ml_dv // 2
    ml_qk = ML_HEADS * ml_dqk
    ql = w_uq.shape[0]
    kvl = w_uk.shape[0]
    o_q = 2 * ml_qk + 2 * ml_width
    o_ig, o_fg = o_q, o_q + ML_HEADS
    o_cq = o_fg + ML_HEADS
    o_ckv = o_cq + ql
    o_kr = o_ckv + kvl
    n_main = o_q + ql + kvl
    tn = 512 if n_main % 512 == 0 else 256
    zeros = lambda n: jnp.zeros((d, n), w_in.dtype)
    w_packed = jnp.concatenate(
        [w_in[:, :o_q], w_in[:, o_cq:o_ckv], w_in[:, o_ckv:o_kr],
         w_in[:, o_kr:o_kr + MLA_ROPE], zeros(GATE_COL - MLA_ROPE),
         w_in[:, o_ig:o_ig + ML_HEADS], w_in[:, o_fg:o_fg + ML_HEADS],
         zeros(tn - GATE_COL - 2 * ML_HEADS)], axis=1).astype(BF16)
    wuq = jnp.concatenate([w_uq, jnp.zeros((ql, MLA_HEADS, HEAD_PAD - MLA_NOPE - MLA_ROPE), w_uq.dtype)],
                          axis=-1).reshape(ql, MLA_HEADS * HEAD_PAD).astype(BF16)
    wuk = w_uk.reshape(kvl, MLA_HEADS * MLA_NOPE).astype(BF16)
    wukt = jnp.transpose(w_uk, (1, 2, 0)).astype(BF16)
    wuvt = jnp.transpose(w_uv, (1, 2, 0)).reshape(MLA_HEADS * MLA_DV, kvl).astype(BF16)
    wuv_h = jnp.transpose(w_uv, (1, 0, 2)).astype(BF16)
    dims = dict(ml_width=ml_width, ml_dv=ml_dv, ml_dqk=ml_dqk, n_main=n_main, tn=tn,
                cq_off=o_q, ckv_off=o_q + ql)
    return dict(w_in=w_packed, wuq=wuq, wuk=wuk, wukt=wukt, wuvt=wuvt, wuv_h=wuv_h,
                w_out=w_out.astype(BF16), w_ff1=w_ff1.astype(BF16), w_ff2=w_ff2.astype(BF16)), dims


def _group(x2, mods, pos, rows_per_group, batch, seq, ml_rows, ml_len, state,
           wts, dims, vecs, invf, tm):
    sh1, sc1 = mods[0], mods[1]
    zm, zs = _inproj(x2, vecs["g_pre1"], sc1, sh1, wts["w_in"], dims["n_main"], dims["tn"],
                     rows_per_group, tm)
    hml, c_new, n_new, m_new = _mlstm(zm, zs, vecs["bg"], vecs["g_mlnorm"], *state, batch, seq,
                                      ml_rows, ml_len, dims["ml_dqk"], dims["ml_dv"])
    q, k, vt, ckv, kr = _mla_prep(zm, zs, pos, invf, vecs["g_cq"], vecs["g_ckv"], wts["wuq"],
                                  wts["wuk"], wts["wuvt"], dims["cq_off"], dims["ckv_off"], tm)
    return hml, (c_new, n_new, m_new), q, k, vt, ckv, kr


def _finish(x2, hml, omla, mods, rows_per_group, wts, vecs, tm):
    _, _, gt1, sh2, sc2, gt2 = mods
    d = x2.shape[1]
    tk = min(512, hml.shape[1])
    x1 = _outproj(hml, omla, wts["w_out"], x2, gt1, vecs["g_post1"], rows_per_group, tm, tk)
    tf = min(256, wts["w_ff1"].shape[1])
    return _ffn(x1, vecs["g_pre2"], sc2, sh2, gt2, vecs["g_post2"], wts["w_ff1"], wts["w_ff2"],
                rows_per_group, tm, tf)


def kernel(x_prompt, x_sample, cache_mla_ckv, cache_mla_krope, state_mlstm_C, state_mlstm_n, state_mlstm_m, c_prompt, c_sample, w_ada, b_ada, g_pre1, g_post1, w_in, b_ig, b_fg, g_mlnorm, g_cq, w_uq, g_ckv, w_uk, w_uv, w_out, g_pre2, g_post2, w_ff1, w_ff2):
    bp, sp, d = x_prompt.shape
    bs, ss, _ = x_sample.shape
    depth = w_in.shape[0]
    past = cache_mla_ckv.shape[2]
    half = MLA_ROPE // 2
    inv = ROPE_THETA ** (-jnp.arange(half, dtype=F32) / half)
    invf = jnp.concatenate([inv, inv, jnp.zeros((LANES - MLA_ROPE,), F32)])[None, :]
    pos_p = jnp.tile(jnp.arange(sp, dtype=F32), bp)[:, None]
    pos_s = jnp.tile(past + jnp.arange(ss, dtype=F32), bs)[:, None]
    tp, ts = bp * sp, bs * ss
    tm_p = min(512, sp)
    tm_s = min(512, ts)
    ml_rows_p = min(256, sp)
    ml_len_s = max(LANES, ss)

    xp = x_prompt.reshape(tp, d)
    xs = x_sample.reshape(ts, d)
    rows_c = bp + bs
    c_all = jnp.concatenate([c_prompt, c_sample, jnp.zeros((-rows_c % 8, d), F32)], axis=0)
    outs = [[] for _ in range(10)]
    for l in range(depth):
        wts, dims = _pack_weights(w_in[l], w_uq[l], w_uk[l], w_uv[l], w_out[l], w_ff1[l], w_ff2[l], d)
        vecs = dict(g_pre1=g_pre1[l][None], g_post1=g_post1[l][None], g_pre2=g_pre2[l][None],
                    g_post2=g_post2[l][None], g_mlnorm=g_mlnorm[l][None], g_cq=g_cq[l][None],
                    g_ckv=g_ckv[l][None], bg=jnp.stack([b_ig[l], b_fg[l]]))
        mod = _modulation(c_all, w_ada[l], b_ada[l][None])
        mods_p = [mod[:bp, i * d:(i + 1) * d] for i in range(N_MOD)]
        mods_s = [mod[bp:rows_c, i * d:(i + 1) * d] for i in range(N_MOD)]

        dqk, dv = dims["ml_dqk"], dims["ml_dv"]
        zero_state = (jnp.zeros((bp, ML_HEADS, dqk, dv), F32), jnp.zeros((bp, ML_HEADS, dqk), F32),
                      jnp.zeros((bp, ML_HEADS), F32))
        hml_p, st_p, q_p, k_p, vt_p, ckv_p, kr_p = _group(
            xp, mods_p, pos_p, sp, bp, sp, ml_rows_p, ml_rows_p, zero_state,
            wts, dims, vecs, invf, tm_p)
        omla_p = _attention(q_p, k_p, vt_p, bp, sp, tm_p)
        xp = _finish(xp, hml_p, omla_p, mods_p, sp, wts, vecs, tm_p)

        state_s = (state_mlstm_C[l], state_mlstm_n[l], state_mlstm_m[l])
        hml_s, st_s, q_s, k_s, _, ckv_s, kr_s = _group(
            xs, mods_s, pos_s, ss, bs, ss, ss, ml_len_s, state_s,
            wts, dims, vecs, invf, tm_s)
        krope_pad = jnp.pad(cache_mla_krope[l], ((0, 0), (0, 0), (0, LANES - MLA_ROPE)))
        omla_s = _sample_mla(q_s, k_s, ckv_s, cache_mla_ckv[l], krope_pad, wts["wukt"],
                             wts["wuv_h"], bs, ss)
        xs = _finish(xs, hml_s, omla_s, mods_s, ss, wts, vecs, tm_s)

        kvl = ckv_p.shape[-1]
        for lst, val in zip(outs, (ckv_p.reshape(bp, sp, kvl), kr_p.reshape(bp, sp, MLA_ROPE), *st_p,
                                   ckv_s.reshape(bs, ss, kvl), kr_s.reshape(bs, ss, MLA_ROPE), *st_s)):
            lst.append(val)
    return (xp.reshape(bp, sp, d), xs.reshape(bs, ss, d), *[jnp.stack(o) for o in outs])
```

```python
import functools
import math

import jax
import jax.numpy as jnp
from jax import lax
from jax.experimental import pallas as pl
from jax.experimental.pallas import tpu as pltpu

CHUNK = 64
EPS = 1e-6
ML_HEADS = 4
MLA_HEADS = 16
MLA_NOPE = 128
MLA_ROPE = 64
MLA_DV = 128
ROPE_THETA = 10000.0
N_MOD = 6
MLA_SCALE = (MLA_NOPE + MLA_ROPE) ** -0.5
Q_SCALE = MLA_SCALE * math.log2(math.e)

LANES = 128
HEAD_PAD = 2 * LANES
SMALL_COLS = 2 * LANES
GATE_COL = LANES
VMEM_LIMIT_BYTES = 60 * 1024 * 1024
NEG = -1e30
GROUP_PAD = LANES
STRIP_ROWS = 128
F32 = jnp.float32
BF16 = jnp.bfloat16


def _cparams(*sem):
    return pltpu.CompilerParams(dimension_semantics=sem, vmem_limit_bytes=VMEM_LIMIT_BYTES)


def _dot(a, b):
    return jnp.dot(a, b, preferred_element_type=F32)


def _dot_nt(a, b):
    return lax.dot_general(a, b, (((1,), (1,)), ((), ())), preferred_element_type=F32)


def _dot_tn(a, b):
    return lax.dot_general(a, b, (((0,), (0,)), ((), ())), preferred_element_type=F32)


def _split3(a):
    hi = a.astype(BF16)
    r1 = a - hi.astype(F32)
    mid = r1.astype(BF16)
    lo = (r1 - mid.astype(F32)).astype(BF16)
    return hi, mid, lo


def _rms(x, g):
    return x * lax.rsqrt(jnp.mean(x * x, axis=-1, keepdims=True) + EPS) * g


def _accumulate_cols(o_ref, a, w_ref, first, width=512):
    @pl.when(first)
    def _():
        o_ref[...] = jnp.zeros(o_ref.shape, o_ref.dtype)

    n = o_ref.shape[1]
    width = min(width, n)
    for c0 in range(0, n, width):
        o_ref[:, c0:c0 + width] += _dot(a, w_ref[:, c0:c0 + width])


def _const_spec(shape):
    nd = len(shape)
    return pl.BlockSpec(shape, lambda *_: (0,) * nd, pipeline_mode=pl.Buffered(1))


def _mod_kernel(c_ref, w_ref, b_ref, o_ref):
    c = c_ref[...]
    a = (c * jax.nn.sigmoid(c)).astype(BF16)
    o_ref[...] = _dot(a, w_ref[...].astype(BF16)) + b_ref[...]


def _modulation(c_all, w_ada, b_ada):
    rows, d = c_all.shape
    n = w_ada.shape[1]
    tn = 512 if n % 512 == 0 else n
    return pl.pallas_call(
        _mod_kernel,
        grid=(n // tn,),
        in_specs=[pl.BlockSpec((rows, d), lambda j: (0, 0)),
                  pl.BlockSpec((d, tn), lambda j: (0, j)),
                  pl.BlockSpec((1, tn), lambda j: (0, j))],
        out_specs=pl.BlockSpec((rows, tn), lambda j: (0, j)),
        out_shape=jax.ShapeDtypeStruct((rows, n), F32),
        compiler_params=_cparams("arbitrary"),
        name="adaln_mod",
    )(c_all, w_ada, b_ada)


def _mod_operand(m, rows_per_group, tm):
    d = m.shape[-1]
    if rows_per_group >= tm:
        bpg = rows_per_group // tm
        return m[:, None, :], pl.BlockSpec((None, 1, d), lambda i, j: (i // bpg, 0, 0))
    gpb = tm // rows_per_group
    m3 = m.reshape(m.shape[0] // gpb, gpb, d)
    m3 = jnp.pad(m3, ((0, 0), (0, GROUP_PAD - gpb), (0, 0)))
    return m3, pl.BlockSpec((None, GROUP_PAD, d), lambda i, j: (i, 0, 0))


def _expand_mod(m_ref, r0, strip, rows_per_group):
    if m_ref.shape[0] == 1:
        return m_ref[...]
    groups = m_ref.shape[0]
    row = lax.broadcasted_iota(jnp.int32, (strip, groups), 0) + r0
    col = lax.broadcasted_iota(jnp.int32, (strip, groups), 1)
    sel = jnp.where(row // rows_per_group == col, 1.0, 0.0).astype(BF16)
    return sum(_dot(sel, part) for part in _split3(m_ref[...]))


def _for_strips(tm, body):
    strip = min(STRIP_ROWS, tm)

    def step(s, carry):
        body(pl.multiple_of(s * strip, strip), strip)
        return carry

    lax.fori_loop(0, tm // strip, step, 0)


def _modulated_norm(x_ref, g_ref, sc_ref, sh_ref, u_ref, rpg):
    def body(r0, strip):
        rows = pl.ds(r0, strip)
        xn = _rms(x_ref[rows, :], g_ref[...])
        u = xn * (1.0 + _expand_mod(sc_ref, r0, strip, rpg)) + _expand_mod(sh_ref, r0, strip, rpg)
        u_ref[rows, :] = u.astype(BF16)

    _for_strips(x_ref.shape[0], body)


def _inproj_kernel(x_ref, g_ref, sc_ref, sh_ref, w_ref, zm_ref, zs_ref, u_ref, *, n_main, rpg):
    j = pl.program_id(1)

    @pl.when(j == 0)
    def _():
        _modulated_norm(x_ref, g_ref, sc_ref, sh_ref, u_ref, rpg)

    r = _dot(u_ref[...], w_ref[...])

    @pl.when(j < n_main)
    def _():
        zm_ref[...] = r.astype(BF16)

    @pl.when(j == n_main)
    def _():
        zs_ref[...] = r[:, :SMALL_COLS]


def _inproj(x, g, sc, sh, w_packed, n_main_cols, tn, rows_per_group, tm):
    t, d = x.shape
    n_main = n_main_cols // tn
    sc_a, sc_s = _mod_operand(sc, rows_per_group, tm)
    sh_a, sh_s = _mod_operand(sh, rows_per_group, tm)
    return pl.pallas_call(
        functools.partial(_inproj_kernel, n_main=n_main, rpg=rows_per_group),
        grid=(t // tm, n_main + 1),
        in_specs=[pl.BlockSpec((tm, d), lambda i, j: (i, 0)),
                  pl.BlockSpec((1, d), lambda i, j: (0, 0)),
                  sc_s, sh_s,
                  pl.BlockSpec((d, tn), lambda i, j: (0, j))],
        out_specs=[pl.BlockSpec((tm, tn), lambda i, j: (i, jnp.minimum(j, n_main - 1))),
                   pl.BlockSpec((tm, SMALL_COLS), lambda i, j: (i, 0))],
        out_shape=[jax.ShapeDtypeStruct((t, n_main_cols), BF16),
                   jax.ShapeDtypeStruct((t, SMALL_COLS), F32)],
        scratch_shapes=[pltpu.VMEM((tm, d), BF16)],
        compiler_params=_cparams("arbitrary", "arbitrary"),
        name="in_proj",
    )(x, g, sc_a, sh_a, w_packed)


def _mlstm_kernel(bg_ref, q_ref, k_ref, v_ref, og_ref, zs_ref, gml_ref, c0_ref, n0_ref, m0_ref,
                  h_ref, c_ref, n_ref, m_ref, *, length, rows, scale):
    head = pl.program_id(1)
    chunk = pl.program_id(2)

    @pl.when(chunk == 0)
    def _():
        c_ref[...] = c0_ref[...]
        n_ref[...] = n0_ref[...]
        m_ref[...] = m0_ref[...]

    def pad(a):
        if rows == length:
            return a
        return jnp.concatenate([a, jnp.zeros((length - rows, a.shape[1]), a.dtype)], axis=0)

    q = pad(q_ref[...])
    k = pad(k_ref[...])
    v = pad(v_ref[...])
    zs = pad(zs_ref[...])

    srow = lax.broadcasted_iota(jnp.int32, (16, SMALL_COLS), 0)
    scol = lax.broadcasted_iota(jnp.int32, (16, SMALL_COLS), 1)
    want = jnp.where(srow < 8, GATE_COL + head, GATE_COL + ML_HEADS + head)
    sel = jnp.where(scol == want, 1.0, 0.0).astype(BF16)
    gates = sum(_dot_nt(sel, part) for part in _split3(zs))
    lane = lax.broadcasted_iota(jnp.int32, (1, length), 1)
    valid = lane < rows
    ig_row = jnp.where(valid, gates[0:1] + bg_ref[0, head], NEG)
    fpre = gates[8:9] + bg_ref[1, head]
    lf_row = jnp.where(valid, jnp.minimum(fpre, 0.0) - jnp.log(1.0 + jnp.exp(-jnp.abs(fpre))), 0.0)

    r_i = lax.broadcasted_iota(jnp.int32, (length, length), 0)
    c_i = lax.broadcasted_iota(jnp.int32, (length, length), 1)
    upper = jnp.where(r_i <= c_i, 1.0, 0.0).astype(BF16)
    lf8 = jnp.broadcast_to(lf_row, (8, length))
    bc_row = sum(_dot(part, upper) for part in _split3(lf8))[0:1]
    eye = jnp.where(r_i == c_i, 1.0, 0.0).astype(BF16)
    stacked = jnp.concatenate([jnp.broadcast_to(bc_row, (LANES // 2, length)),
                               jnp.broadcast_to(ig_row, (LANES // 2, length))], axis=0)
    cols = sum(_dot_nt(eye, part) for part in _split3(stacked))
    bc_col = cols[:, 0:1]
    ig_col = cols[:, LANES // 2:LANES // 2 + 1]

    m_prev = m_ref[...]
    dmat = jnp.where(r_i >= c_i, bc_col - bc_row + ig_row, NEG)
    a_col = bc_col + m_prev
    m_t = jnp.maximum(a_col, jnp.max(dmat, axis=1, keepdims=True))
    w_intra = jnp.exp(dmat - m_t)
    w_inter = jnp.exp(a_col - m_t)
    sc = _dot_nt(q, k) * scale * w_intra
    c_state = c_ref[...]
    n_state = n_ref[...]
    num = _dot(sc.astype(BF16), v) + w_inter * _dot(q, c_state.astype(BF16))
    qn = jnp.sum(q.astype(F32) * n_state, axis=1, keepdims=True)
    nq = jnp.sum(sc, axis=1, keepdims=True) + w_inter * qn
    hh = num / jnp.maximum(jnp.abs(nq), jnp.exp(-m_t))

    hv = hh[:rows]
    hn = _rms(hv, gml_ref[...])
    h_ref[...] = (hn * jax.nn.sigmoid(og_ref[...].astype(F32))).astype(BF16)

    m_new = m_t[length - 1:length, :]
    bc_last = bc_row[:, length - 1:length]
    g_state = jnp.exp(bc_last + m_prev - m_new)
    g_tok = jnp.exp(bc_last - bc_col + ig_col - m_new)
    kg = k.astype(F32) * (scale * g_tok)
    c_ref[...] = g_state * c_state + _dot_tn(kg.astype(BF16), v)
    n_ref[...] = g_state * n_state + jnp.sum(kg, axis=0, keepdims=True)
    m_ref[...] = m_new


def _mlstm(zm, zs, bg, g_mlnorm, c0, n0, m0, batch, seq, rows, length, dqk, dv):
    t = zm.shape[0]
    nc = seq // rows
    hh = ML_HEADS
    kernel = functools.partial(_mlstm_kernel, length=length, rows=rows, scale=dqk ** -0.5)
    row = lambda b, h, c: b * nc + c
    st4 = lambda b, h, c: (b, h, 0, 0)
    outs = pl.pallas_call(
        kernel,
        grid=(batch, hh, nc),
        in_specs=[pl.BlockSpec(memory_space=pltpu.SMEM),
                  pl.BlockSpec((rows, dqk), lambda b, h, c: (row(b, h, c), h)),
                  pl.BlockSpec((rows, dqk), lambda b, h, c: (row(b, h, c), hh + h)),
                  pl.BlockSpec((rows, dv), lambda b, h, c: (row(b, h, c), hh + h)),
                  pl.BlockSpec((rows, dv), lambda b, h, c: (row(b, h, c), 2 * hh + h)),
                  pl.BlockSpec((rows, SMALL_COLS), lambda b, h, c: (row(b, h, c), 0)),
                  pl.BlockSpec((1, dv), lambda b, h, c: (0, h)),
                  pl.BlockSpec((None, None, dqk, dv), st4),
                  pl.BlockSpec((None, None, 1, dqk), st4),
                  pl.BlockSpec((None, None, 1, 1), st4)],
        out_specs=[pl.BlockSpec((rows, dv), lambda b, h, c: (row(b, h, c), h)),
                   pl.BlockSpec((None, None, dqk, dv), st4),
                   pl.BlockSpec((None, None, 1, dqk), st4),
                   pl.BlockSpec((None, None, 1, 1), st4)],
        out_shape=[jax.ShapeDtypeStruct((t, hh * dv), BF16),
                   jax.ShapeDtypeStruct((batch, hh, dqk, dv), F32),
                   jax.ShapeDtypeStruct((batch, hh, 1, dqk), F32),
                   jax.ShapeDtypeStruct((batch, hh, 1, 1), F32)],
        compiler_params=_cparams("arbitrary", "arbitrary", "arbitrary"),
        name="mlstm",
    )(bg, zm, zm, zm, zm, zs, g_mlnorm, c0, n0[:, :, None, :], m0[:, :, None, None])
    h, c, n, m = outs
    return h, c, n[:, :, 0, :], m[:, :, 0, 0]


def _mla_prep_kernel(cq_ref, ckv_ref, zs_ref, pos_ref, invf_ref, gcq_ref, gckv_ref,
                     wuq_ref, wuk_ref, wuvt_ref,
                     q_ref, k_ref, vt_ref, ckv_out_ref, kr_out_ref):
    tm = cq_ref.shape[0]
    ang = pos_ref[...] * invf_ref[...]
    lane = lax.broadcasted_iota(jnp.int32, (tm, LANES), 1)
    half = MLA_ROPE // 2
    cosv = jnp.cos(ang)
    sinv = jnp.sin(ang)
    c_t = jnp.where(lane < MLA_ROPE, cosv, 0.0)
    s_lo = jnp.where(lane < half, -sinv, 0.0)
    s_hi = jnp.where((lane >= half) & (lane < MLA_ROPE), sinv, 0.0)

    def rope(x):
        return x * c_t + pltpu.roll(x, LANES - half, 1) * s_lo + pltpu.roll(x, half, 1) * s_hi

    cqn = _rms(cq_ref[...].astype(F32), gcq_ref[...]).astype(BF16)
    for h in range(MLA_HEADS):
        qh = _dot(cqn, wuq_ref[:, h * HEAD_PAD:(h + 1) * HEAD_PAD])
        q_ref[h, :, 0:LANES] = (qh[:, 0:LANES] * Q_SCALE).astype(BF16)
        q_ref[h, :, LANES:HEAD_PAD] = (rope(qh[:, LANES:HEAD_PAD]) * Q_SCALE).astype(BF16)

    ckvn = _rms(ckv_ref[...].astype(F32), gckv_ref[...])
    ckv_out_ref[...] = ckvn
    cb = ckvn.astype(BF16)
    kr = rope(zs_ref[:, 0:LANES])
    kr_out_ref[...] = kr[:, 0:MLA_ROPE]
    krb = kr.astype(BF16)
    kn = _dot(cb, wuk_ref[...])
    for h in range(MLA_HEADS):
        k_ref[h, :, 0:LANES] = kn[:, h * MLA_NOPE:(h + 1) * MLA_NOPE].astype(BF16)
        k_ref[h, :, LANES:HEAD_PAD] = krb
    vt = _dot_nt(wuvt_ref[...], cb)
    vt_ref[...] = vt.reshape(MLA_HEADS, MLA_DV, tm).astype(BF16)


def _mla_prep(zm, zs, pos, invf, g_cq, g_ckv, wuq, wuk, wuvt, cq_off, ckv_off, tm):
    t = zm.shape[0]
    ql = g_cq.shape[-1]
    kvl = g_ckv.shape[-1]
    hh = MLA_HEADS
    return pl.pallas_call(
        _mla_prep_kernel,
        grid=(t // tm,),
        in_specs=[pl.BlockSpec((tm, ql), lambda i: (i, cq_off // ql)),
                  pl.BlockSpec((tm, kvl), lambda i: (i, ckv_off // kvl)),
                  pl.BlockSpec((tm, SMALL_COLS), lambda i: (i, 0)),
                  pl.BlockSpec((tm, 1), lambda i: (i, 0)),
                  _const_spec((1, LANES)),
                  _const_spec((1, ql)),
                  _const_spec((1, kvl)),
                  _const_spec(wuq.shape),
                  _const_spec(wuk.shape),
                  _const_spec(wuvt.shape)],
        out_specs=[pl.BlockSpec((hh, tm, HEAD_PAD), lambda i: (0, i, 0)),
                   pl.BlockSpec((hh, tm, HEAD_PAD), lambda i: (0, i, 0)),
                   pl.BlockSpec((hh, None, MLA_DV, tm), lambda i: (0, i, 0, 0)),
                   pl.BlockSpec((tm, kvl), lambda i: (i, 0)),
                   pl.BlockSpec((tm, MLA_ROPE), lambda i: (i, 0))],
        out_shape=[jax.ShapeDtypeStruct((hh, t, HEAD_PAD), BF16),
                   jax.ShapeDtypeStruct((hh, t, HEAD_PAD), BF16),
                   jax.ShapeDtypeStruct((hh, t // tm, MLA_DV, tm), BF16),
                   jax.ShapeDtypeStruct((t, kvl), F32),
                   jax.ShapeDtypeStruct((t, MLA_ROPE), F32)],
        compiler_params=_cparams("arbitrary"),
        name="mla_prep",
    )(zm, zm, zs, pos, invf, g_cq, g_ckv, wuq, wuk, wuvt)


def _attn_kernel(q_ref, k_ref, vt_ref, o_ref, acc_ref, m_ref, l_ref, *, seq, tq):
    nq = seq // tq
    tv = vt_ref.shape[-1]
    sub = tq // tv

    def q_body(qi, carry):
        q0 = pl.multiple_of(qi * tq, tq)
        q = q_ref[pl.ds(q0, tq), :]
        m_ref[...] = jnp.full(m_ref.shape, NEG, F32)
        l_ref[...] = jnp.zeros(l_ref.shape, F32)
        acc_ref[...] = jnp.zeros(acc_ref.shape, F32)

        def kv_step(ki, masked):
            k0 = pl.multiple_of(ki * tq, tq)
            kb = k_ref[pl.ds(k0, tq), :]
            s = _dot_nt(kb, q)
            if masked:
                kc = lax.broadcasted_iota(jnp.int32, s.shape, 0) // CHUNK
                qc = lax.broadcasted_iota(jnp.int32, s.shape, 1) // CHUNK
                s = jnp.where(kc <= qc, s, NEG)
            m_prev = m_ref[...]
            m_new = jnp.maximum(m_prev, jnp.max(s, axis=0, keepdims=True))
            alpha = jnp.exp2(m_prev - m_new)
            p = jnp.exp2(s - m_new)
            l_ref[...] = alpha * l_ref[...] + jnp.sum(p, axis=0, keepdims=True)
            pb = p.astype(BF16)
            pv = _dot(vt_ref[ki * sub], pb[0:tv])
            for j in range(1, sub):
                pv = pv + _dot(vt_ref[ki * sub + j], pb[j * tv:(j + 1) * tv])
            acc_ref[...] = alpha * acc_ref[...] + pv
            m_ref[...] = m_new

        def loop_body(ki, c):
            kv_step(ki, False)
            return c

        lax.fori_loop(0, qi, loop_body, 0)
        kv_step(qi, True)
        out = acc_ref[...] * (1.0 / l_ref[...])
        o_ref[pl.ds(q0, tq), :] = out.T.astype(BF16)
        return carry

    lax.fori_loop(0, nq, q_body, 0)


def _attention(q, k, vt, batch, seq, tq):
    hh = MLA_HEADS
    tv = vt.shape[-1]
    nblk = seq // tv
    return pl.pallas_call(
        functools.partial(_attn_kernel, seq=seq, tq=tq),
        grid=(batch, hh),
        in_specs=[pl.BlockSpec((None, seq, HEAD_PAD), lambda b, h: (h, b, 0)),
                  pl.BlockSpec((None, seq, HEAD_PAD), lambda b, h: (h, b, 0)),
                  pl.BlockSpec((None, nblk, MLA_DV, tv), lambda b, h: (h, b, 0, 0))],
        out_specs=pl.BlockSpec((seq, MLA_DV), lambda b, h: (b, h)),
        out_shape=jax.ShapeDtypeStruct((batch * seq, hh * MLA_DV), BF16),
        scratch_shapes=[pltpu.VMEM((MLA_DV, tq), F32),
                        pltpu.VMEM((1, tq), F32),
                        pltpu.VMEM((1, tq), F32)],
        compiler_params=_cparams("arbitrary", "arbitrary"),
        name="mla_attention",
    )(q, k, vt)


def _absorb_kernel(q_ref, wukt_ref, o_ref, *, kvl):
    q = q_ref[...]
    ql = _dot(q[:, 0:MLA_NOPE], wukt_ref[...])
    full = jnp.concatenate([ql.astype(BF16), q[:, LANES:HEAD_PAD]], axis=1)
    o_ref[...] = full.reshape(o_ref.shape)


def _sample_attn_kernel(q_ref, cpast_ref, kpast_ref, cnew_ref, knew_ref, o_ref, *, kvl, new):
    qb = q_ref[...]
    ql = qb[:, 0:kvl]
    qr = qb[:, kvl:kvl + LANES]
    c_past = cpast_ref[...].astype(BF16)
    k_past = kpast_ref[...].astype(BF16)
    s_past = _dot_nt(ql, c_past) + _dot_nt(qr, k_past)
    fill = LANES - new
    c_new = jnp.concatenate([cnew_ref[...].astype(BF16), jnp.zeros((fill, kvl), BF16)], axis=0)
    k_new = jnp.concatenate([knew_ref[...], jnp.zeros((fill, LANES), BF16)], axis=0)
    s_new = _dot_nt(ql, c_new) + _dot_nt(qr, k_new)
    lane = lax.broadcasted_iota(jnp.int32, s_new.shape, 1)
    s_new = jnp.where(lane < new, s_new, NEG)
    m = jnp.maximum(jnp.max(s_past, axis=1, keepdims=True), jnp.max(s_new, axis=1, keepdims=True))
    p_past = jnp.exp2(s_past - m)
    p_new = jnp.exp2(s_new - m)
    l = jnp.sum(p_past, axis=1, keepdims=True) + jnp.sum(p_new, axis=1, keepdims=True)
    lat = _dot(p_past.astype(BF16), c_past) + _dot(p_new.astype(BF16), c_new)
    o_ref[...] = (lat / l).astype(BF16)


def _uv_kernel(lat_ref, wuv_ref, o_ref):
    lat = lat_ref[...]
    lat2 = lat.reshape(lat.shape[0] * lat.shape[1], lat.shape[2])
    o_ref[...] = _dot(lat2, wuv_ref[...]).astype(BF16)


def _sample_mla(q_s, k_s, ckv_new, cache_ckv, cache_krope_pad, wukt, wuv_h, batch, new):
    hh = MLA_HEADS
    t = batch * new
    kvl = ckv_new.shape[-1]
    width = kvl + LANES
    past = cache_ckv.shape[1]
    q_abs = pl.pallas_call(
        functools.partial(_absorb_kernel, kvl=kvl),
        grid=(hh,),
        in_specs=[pl.BlockSpec((None, t, HEAD_PAD), lambda h: (h, 0, 0)),
                  pl.BlockSpec((None, MLA_NOPE, kvl), lambda h: (h, 0, 0))],
        out_specs=pl.BlockSpec((batch, new, width), lambda h: (0, h, 0)),
        out_shape=jax.ShapeDtypeStruct((batch, hh * new, width), BF16),
        compiler_params=_cparams("arbitrary"),
        name="mla_absorb",
    )(q_s, wukt)
    lat = pl.pallas_call(
        functools.partial(_sample_attn_kernel, kvl=kvl, new=new),
        grid=(batch,),
        in_specs=[pl.BlockSpec((None, hh * new, width), lambda b: (b, 0, 0)),
                  pl.BlockSpec((None, past, kvl), lambda b: (b, 0, 0)),
                  pl.BlockSpec((None, past, LANES), lambda b: (b, 0, 0)),
                  pl.BlockSpec((new, kvl), lambda b: (b, 0)),
                  pl.BlockSpec((None, new, LANES), lambda b: (0, b, 1))],
        out_specs=pl.BlockSpec((None, hh * new, kvl), lambda b: (b, 0, 0)),
        out_shape=jax.ShapeDtypeStruct((batch, hh * new, kvl), BF16),
        compiler_params=_cparams("arbitrary"),
        name="mla_sample_attention",
    )(q_abs, cache_ckv, cache_krope_pad, ckv_new, k_s)
    return pl.pallas_call(
        _uv_kernel,
        grid=(hh,),
        in_specs=[pl.BlockSpec((batch, new, kvl), lambda h: (0, h, 0)),
                  pl.BlockSpec((None, kvl, MLA_DV), lambda h: (h, 0, 0))],
        out_specs=pl.BlockSpec((t, MLA_DV), lambda h: (0, h)),
        out_shape=jax.ShapeDtypeStruct((t, hh * MLA_DV), BF16),
        compiler_params=_cparams("arbitrary"),
        name="mla_uv",
    )(lat, wuv_h)


def _outproj_kernel(hm_ref, om_ref, w_ref, x_ref, gt_ref, g_ref, gpre2_ref, sc2_ref, sh2_ref,
                    o_ref, u_ref, *, half, rpg):
    kk = pl.program_id(1)
    a = jnp.where(kk < half, hm_ref[...], om_ref[...])
    _accumulate_cols(o_ref, a, w_ref, kk == 0)

    @pl.when(kk == 2 * half - 1)
    def _():
        def body(r0, strip):
            rows = pl.ds(r0, strip)
            y = _rms(o_ref[rows, :], g_ref[...])
            x1 = x_ref[rows, :] + _expand_mod(gt_ref, r0, strip, rpg) * y
            o_ref[rows, :] = x1
            u = (_rms(x1, gpre2_ref[...]) * (1.0 + _expand_mod(sc2_ref, r0, strip, rpg))
                 + _expand_mod(sh2_ref, r0, strip, rpg))
            u_ref[rows, :] = u.astype(BF16)

        _for_strips(x_ref.shape[0], body)


def _outproj(hml, omla, w_out, x, gt, g, gpre2, sc2, sh2, rows_per_group, tm, tk):
    t, d = x.shape
    half = hml.shape[1] // tk
    gt_a, gt_s = _mod_operand(gt, rows_per_group, tm)
    sc_a, sc_s = _mod_operand(sc2, rows_per_group, tm)
    sh_a, sh_s = _mod_operand(sh2, rows_per_group, tm)
    vec = pl.BlockSpec((1, d), lambda i, k: (0, 0))
    return pl.pallas_call(
        functools.partial(_outproj_kernel, half=half, rpg=rows_per_group),
        grid=(t // tm, 2 * half),
        in_specs=[pl.BlockSpec((tm, tk), lambda i, k: (i, jnp.minimum(k, half - 1))),
                  pl.BlockSpec((tm, tk), lambda i, k: (i, jnp.maximum(k - half, 0))),
                  pl.BlockSpec((tk, d), lambda i, k: (k, 0)),
                  pl.BlockSpec((tm, d), lambda i, k: (i, 0)),
                  gt_s, vec, vec, sc_s, sh_s],
        out_specs=[pl.BlockSpec((tm, d), lambda i, k: (i, 0)),
                   pl.BlockSpec((tm, d), lambda i, k: (i, 0))],
        out_shape=[jax.ShapeDtypeStruct((t, d), F32),
                   jax.ShapeDtypeStruct((t, d), BF16)],
        compiler_params=_cparams("arbitrary", "arbitrary"),
        name="out_proj",
    )(hml, omla, w_out, x, gt_a, g, gpre2, sc_a, sh_a)


def _ffn_kernel(u_ref, gt_ref, gpost_ref, w1_ref, w2_ref, x_hbm, o_hbm,
                acc_ref, xs_ref, sem_in, sem_out, *, rpg):
    i = pl.program_id(0)
    f = pl.program_id(1)
    tm = acc_ref.shape[0]
    strip = xs_ref.shape[1]

    h = jnp.maximum(_dot(u_ref[...], w1_ref[...]), 0.0)
    _accumulate_cols(acc_ref, (h * h).astype(BF16), w2_ref, f == 0)

    @pl.when(f == pl.num_programs(1) - 1)
    def _():
        n = tm // strip
        row0 = i * tm

        def x_copy(s, slot):
            return pltpu.make_async_copy(x_hbm.at[pl.ds(row0 + s * strip, strip), :],
                                         xs_ref.at[slot], sem_in.at[slot])

        def o_copy(s):
            return pltpu.make_async_copy(acc_ref.at[pl.ds(s * strip, strip), :],
                                         o_hbm.at[pl.ds(row0 + s * strip, strip), :],
                                         sem_out.at[s])

        x_copy(0, 0).start()
        for s in range(n):
            slot = s % 2
            if s + 1 < n:
                x_copy(s + 1, 1 - slot).start()
            x_copy(s, slot).wait()
            rows = pl.ds(s * strip, strip)
            y = _rms(acc_ref[rows, :], gpost_ref[...])
            acc_ref[rows, :] = xs_ref[slot] + _expand_mod(gt_ref, s * strip, strip, rpg) * y
            o_copy(s).start()
        for s in range(n):
            o_copy(s).wait()


def _ffn(x, u, gt, gpost, w1, w2, rows_per_group, tm, tf):
    t, d = x.shape
    ff = w1.shape[1]
    strip = min(STRIP_ROWS, tm)
    gt_a, gt_s = _mod_operand(gt, rows_per_group, tm)
    return pl.pallas_call(
        functools.partial(_ffn_kernel, rpg=rows_per_group),
        grid=(t // tm, ff // tf),
        in_specs=[pl.BlockSpec((tm, d), lambda i, f: (i, 0), pipeline_mode=pl.Buffered(1)),
                  gt_s,
                  pl.BlockSpec((1, d), lambda i, f: (0, 0)),
                  pl.BlockSpec((d, tf), lambda i, f: (0, f)),
                  pl.BlockSpec((tf, d), lambda i, f: (f, 0)),
                  pl.BlockSpec(memory_space=pl.ANY)],
        out_specs=pl.BlockSpec(memory_space=pl.ANY),
        out_shape=jax.ShapeDtypeStruct((t, d), F32),
        scratch_shapes=[pltpu.VMEM((tm, d), F32),
                        pltpu.VMEM((2, strip, d), F32),
                        pltpu.SemaphoreType.DMA((2,)),
                        pltpu.SemaphoreType.DMA((tm // strip,))],
        compiler_params=_cparams("arbitrary", "arbitrary"),
        name="ffn",
    )(u, gt_a, gpost, w1, w2, x)


def _pack_weights(w_in, w_uq, w_uk, w_uv, w_out, w_ff1, w_ff2, d):
    ml_width = d // 2
    ml_dv = ml_width // ML_HEADS
    ml_dqk = ml_dv // 2
    ml_qk = ML_HEADS * ml_dqk
    ql = w_uq.shape[0]
    kvl = w_uk.shape[0]
    o_q = 2 * ml_qk + 2 * ml_width
    o_ig, o_fg = o_q, o_q + ML_HEADS
    o_cq = o_fg + ML_HEADS
    o_ckv = o_cq + ql
    o_kr = o_ckv + kvl
    n_main = o_q + ql + kvl
    tn = 512 if n_main % 512 == 0 else 256
    zeros = lambda n: jnp.zeros((d, n), w_in.dtype)
    w_packed = jnp.concatenate(
        [w_in[:, :o_q], w_in[:, o_cq:o_ckv], w_in[:, o_ckv:o_kr],
         w_in[:, o_kr:o_kr + MLA_ROPE], zeros(GATE_COL - MLA_ROPE),
         w_in[:, o_ig:o_ig + ML_HEADS], w_in[:, o_fg:o_fg + ML_HEADS],
         zeros(tn - GATE_COL - 2 * ML_HEADS)], axis=1).astype(BF16)
    wuq = jnp.concatenate([w_uq, jnp.zeros((ql, MLA_HEADS, HEAD_PAD - MLA_NOPE - MLA_ROPE), w_uq.dtype)],
                          axis=-1).reshape(ql, MLA_HEADS * HEAD_PAD).astype(BF16)
    wuk = w_uk.reshape(kvl, MLA_HEADS * MLA_NOPE).astype(BF16)
    wukt = jnp.transpose(w_uk, (1, 2, 0)).astype(BF16)
    wuvt = jnp.transpose(w_uv, (1, 2, 0)).reshape(MLA_HEADS * MLA_DV, kvl).astype(BF16)
    wuv_h = jnp.transpose(w_uv, (1, 0, 2)).astype(BF16)
    dims = dict(ml_width=ml_width, ml_dv=ml_dv, ml_dqk=ml_dqk, n_main=n_main, tn=tn,
                cq_off=o_q, ckv_off=o_q + ql)
    return dict(w_in=w_packed, wuq=wuq, wuk=wuk, wukt=wukt, wuvt=wuvt, wuv_h=wuv_h,
                w_out=w_out.astype(BF16), w_ff1=w_ff1.astype(BF16), w_ff2=w_ff2.astype(BF16)), dims


def _group(x2, mods, pos, rows_per_group, batch, seq, ml_rows, ml_len, state,
           wts, dims, vecs, invf, tm):
    sh1, sc1 = mods[0], mods[1]
    zm, zs = _inproj(x2, vecs["g_pre1"], sc1, sh1, wts["w_in"], dims["n_main"], dims["tn"],
                     rows_per_group, tm)
    hml, c_new, n_new, m_new = _mlstm(zm, zs, vecs["bg"], vecs["g_mlnorm"], *state, batch, seq,
                                      ml_rows, ml_len, dims["ml_dqk"], dims["ml_dv"])
    q, k, vt, ckv, kr = _mla_prep(zm, zs, pos, invf, vecs["g_cq"], vecs["g_ckv"], wts["wuq"],
                                  wts["wuk"], wts["wuvt"], dims["cq_off"], dims["ckv_off"], tm)
    return hml, (c_new, n_new, m_new), q, k, vt, ckv, kr


def _finish(x2, hml, omla, mods, rows_per_group, wts, vecs, tm):
    _, _, gt1, sh2, sc2, gt2 = mods
    d = x2.shape[1]
    tk = min(512, hml.shape[1])
    x1, u2 = _outproj(hml, omla, wts["w_out"], x2, gt1, vecs["g_post1"], vecs["g_pre2"], sc2, sh2,
                      rows_per_group, tm, tk)
    tf = min(1024, wts["w_ff1"].shape[1])
    return _ffn(x1, u2, gt2, vecs["g_post2"], wts["w_ff1"], wts["w_ff2"], rows_per_group, tm, tf)


def kernel(x_prompt, x_sample, cache_mla_ckv, cache_mla_krope, state_mlstm_C, state_mlstm_n, state_mlstm_m, c_prompt, c_sample, w_ada, b_ada, g_pre1, g_post1, w_in, b_ig, b_fg, g_mlnorm, g_cq, w_uq, g_ckv, w_uk, w_uv, w_out, g_pre2, g_post2, w_ff1, w_ff2):
    bp, sp, d = x_prompt.shape
    bs, ss, _ = x_sample.shape
    depth = w_in.shape[0]
    past = cache_mla_ckv.shape[2]
    half = MLA_ROPE // 2
    inv = ROPE_THETA ** (-jnp.arange(half, dtype=F32) / half)
    invf = jnp.concatenate([inv, inv, jnp.zeros((LANES - MLA_ROPE,), F32)])[None, :]
    pos_p = jnp.tile(jnp.arange(sp, dtype=F32), bp)[:, None]
    pos_s = jnp.tile(past + jnp.arange(ss, dtype=F32), bs)[:, None]
    tp, ts = bp * sp, bs * ss
    tm_p = min(512, sp)
    tm_s = min(512, ts)
    ml_rows_p = min(256, sp)
    ml_len_s = max(LANES, ss)

    xp = x_prompt.reshape(tp, d)
    xs = x_sample.reshape(ts, d)
    rows_c = bp + bs
    c_all = jnp.concatenate([c_prompt, c_sample, jnp.zeros((-rows_c % 8, d), F32)], axis=0)
    outs = [[] for _ in range(10)]
    for l in range(depth):
        wts, dims = _pack_weights(w_in[l], w_uq[l], w_uk[l], w_uv[l], w_out[l], w_ff1[l], w_ff2[l], d)
        vecs = dict(g_pre1=g_pre1[l][None], g_post1=g_post1[l][None], g_pre2=g_pre2[l][None],
                    g_post2=g_post2[l][None], g_mlnorm=g_mlnorm[l][None], g_cq=g_cq[l][None],
                    g_ckv=g_ckv[l][None], bg=jnp.stack([b_ig[l], b_fg[l]]))
        mod = _modulation(c_all, w_ada[l], b_ada[l][None])
        mods_p = [mod[:bp, i * d:(i + 1) * d] for i in range(N_MOD)]
        mods_s = [mod[bp:rows_c, i * d:(i + 1) * d] for i in range(N_MOD)]

        dqk, dv = dims["ml_dqk"], dims["ml_dv"]
        zero_state = (jnp.zeros((bp, ML_HEADS, dqk, dv), F32), jnp.zeros((bp, ML_HEADS, dqk), F32),
                      jnp.zeros((bp, ML_HEADS), F32))
        hml_p, st_p, q_p, k_p, vt_p, ckv_p, kr_p = _group(
            xp, mods_p, pos_p, sp, bp, sp, ml_rows_p, ml_rows_p, zero_state,
            wts, dims, vecs, invf, tm_p)
        omla_p = _attention(q_p, k_p, vt_p, bp, sp, min(1024, sp))
        xp = _finish(xp, hml_p, omla_p, mods_p, sp, wts, vecs, tm_p)

        state_s = (state_mlstm_C[l], state_mlstm_n[l], state_mlstm_m[l])
        hml_s, st_s, q_s, k_s, _, ckv_s, kr_s = _group(
            xs, mods_s, pos_s, ss, bs, ss, ss, ml_len_s, state_s,
            wts, dims, vecs, invf, tm_s)
        krope_pad = jnp.pad(cache_mla_krope[l], ((0, 0), (0, 0), (0, LANES - MLA_ROPE)))
        omla_s = _sample_mla(q_s, k_s, ckv_s, cache_mla_ckv[l], krope_pad, wts["wukt"],
                             wts["wuv_h"], bs, ss)
        xs = _finish(xs, hml_s, omla_s, mods_s, ss, wts, vecs, tm_s)

        kvl = ckv_p.shape[-1]
        for lst, val in zip(outs, (ckv_p.reshape(bp, sp, kvl), kr_p.reshape(bp, sp, MLA_ROPE), *st_p,
                                   ckv_s.reshape(bs, ss, kvl), kr_s.reshape(bs, ss, MLA_ROPE), *st_s)):
            lst.append(val)
    return (xp.reshape(bp, sp, d), xs.reshape(bs, ss, d), *[jnp.stack(o) for o in outs])
```

```python
import functools
import math

import jax
import jax.numpy as jnp
from jax import lax
from jax.experimental import pallas as pl
from jax.experimental.pallas import tpu as pltpu

CHUNK = 64
EPS = 1e-6
ML_HEADS = 4
MLA_HEADS = 16
MLA_NOPE = 128
MLA_ROPE = 64
MLA_DV = 128
ROPE_THETA = 10000.0
N_MOD = 6
MLA_SCALE = (MLA_NOPE + MLA_ROPE) ** -0.5
Q_SCALE = MLA_SCALE * math.log2(math.e)

LANES = 128
HEAD_PAD = 2 * LANES
SMALL_COLS = 2 * LANES
GATE_COL = LANES
VMEM_LIMIT_BYTES = 60 * 1024 * 1024
NEG = -1e30
GROUP_PAD = LANES
STRIP_ROWS = 128
F32 = jnp.float32
BF16 = jnp.bfloat16


def _cparams(*sem):
    return pltpu.CompilerParams(dimension_semantics=sem, vmem_limit_bytes=VMEM_LIMIT_BYTES)


def _dot(a, b):
    return jnp.dot(a, b, preferred_element_type=F32)


def _dot_nt(a, b):
    return lax.dot_general(a, b, (((1,), (1,)), ((), ())), preferred_element_type=F32)


def _dot_tn(a, b):
    return lax.dot_general(a, b, (((0,), (0,)), ((), ())), preferred_element_type=F32)


def _split3(a):
    hi = a.astype(BF16)
    r1 = a - hi.astype(F32)
    mid = r1.astype(BF16)
    lo = (r1 - mid.astype(F32)).astype(BF16)
    return hi, mid, lo


def _rms(x, g):
    return x * lax.rsqrt(jnp.mean(x * x, axis=-1, keepdims=True) + EPS) * g


def _accumulate_cols(o_ref, a, w_ref, first, width=512):
    @pl.when(first)
    def _():
        o_ref[...] = jnp.zeros(o_ref.shape, o_ref.dtype)

    n = o_ref.shape[1]
    width = min(width, n)
    for c0 in range(0, n, width):
        o_ref[:, c0:c0 + width] += _dot(a, w_ref[:, c0:c0 + width])


def _const_spec(shape):
    nd = len(shape)
    return pl.BlockSpec(shape, lambda *_: (0,) * nd, pipeline_mode=pl.Buffered(1))


def _mod_kernel(c_ref, w_ref, b_ref, o_ref):
    c = c_ref[...]
    a = (c * jax.nn.sigmoid(c)).astype(BF16)
    o_ref[...] = _dot(a, w_ref[...].astype(BF16)) + b_ref[...]


def _modulation(c_all, w_ada, b_ada):
    rows, d = c_all.shape
    n = w_ada.shape[1]
    tn = 512 if n % 512 == 0 else n
    return pl.pallas_call(
        _mod_kernel,
        grid=(n // tn,),
        in_specs=[pl.BlockSpec((rows, d), lambda j: (0, 0)),
                  pl.BlockSpec((d, tn), lambda j: (0, j)),
                  pl.BlockSpec((1, tn), lambda j: (0, j))],
        out_specs=pl.BlockSpec((rows, tn), lambda j: (0, j)),
        out_shape=jax.ShapeDtypeStruct((rows, n), F32),
        compiler_params=_cparams("arbitrary"),
        name="adaln_mod",
    )(c_all, w_ada, b_ada)


def _mod_operand(m, rows_per_group, tm):
    d = m.shape[-1]
    if rows_per_group >= tm:
        bpg = rows_per_group // tm
        return m[:, None, :], pl.BlockSpec((None, 1, d), lambda i, j: (i // bpg, 0, 0))
    gpb = tm // rows_per_group
    m3 = m.reshape(m.shape[0] // gpb, gpb, d)
    m3 = jnp.pad(m3, ((0, 0), (0, GROUP_PAD - gpb), (0, 0)))
    return m3, pl.BlockSpec((None, GROUP_PAD, d), lambda i, j: (i, 0, 0))


def _expand_mod(m_ref, r0, strip, rows_per_group):
    if m_ref.shape[0] == 1:
        return m_ref[...]
    groups = m_ref.shape[0]
    row = lax.broadcasted_iota(jnp.int32, (strip, groups), 0) + r0
    col = lax.broadcasted_iota(jnp.int32, (strip, groups), 1)
    sel = jnp.where(row // rows_per_group == col, 1.0, 0.0).astype(BF16)
    return sum(_dot(sel, part) for part in _split3(m_ref[...]))


def _for_strips(tm, body):
    strip = min(STRIP_ROWS, tm)

    def step(s, carry):
        body(pl.multiple_of(s * strip, strip), strip)
        return carry

    lax.fori_loop(0, tm // strip, step, 0)


def _modulated_norm(x_ref, g_ref, sc_ref, sh_ref, u_ref, rpg):
    def body(r0, strip):
        rows = pl.ds(r0, strip)
        xn = _rms(x_ref[rows, :], g_ref[...])
        u = xn * (1.0 + _expand_mod(sc_ref, r0, strip, rpg)) + _expand_mod(sh_ref, r0, strip, rpg)
        u_ref[rows, :] = u.astype(BF16)

    _for_strips(x_ref.shape[0], body)


def _inproj_kernel(x_ref, g_ref, sc_ref, sh_ref, w_ref, zm_ref, zs_ref, u_ref, *, n_main, rpg):
    j = pl.program_id(1)

    @pl.when(j == 0)
    def _():
        _modulated_norm(x_ref, g_ref, sc_ref, sh_ref, u_ref, rpg)

    r = _dot(u_ref[...], w_ref[...])

    @pl.when(j < n_main)
    def _():
        zm_ref[...] = r.astype(BF16)

    @pl.when(j == n_main)
    def _():
        zs_ref[...] = r[:, :SMALL_COLS]


def _inproj(x, g, sc, sh, w_packed, n_main_cols, tn, rows_per_group, tm):
    t, d = x.shape
    n_main = n_main_cols // tn
    sc_a, sc_s = _mod_operand(sc, rows_per_group, tm)
    sh_a, sh_s = _mod_operand(sh, rows_per_group, tm)
    return pl.pallas_call(
        functools.partial(_inproj_kernel, n_main=n_main, rpg=rows_per_group),
        grid=(t // tm, n_main + 1),
        in_specs=[pl.BlockSpec((tm, d), lambda i, j: (i, 0)),
                  pl.BlockSpec((1, d), lambda i, j: (0, 0)),
                  sc_s, sh_s,
                  pl.BlockSpec((d, tn), lambda i, j: (0, j))],
        out_specs=[pl.BlockSpec((tm, tn), lambda i, j: (i, jnp.minimum(j, n_main - 1))),
                   pl.BlockSpec((tm, SMALL_COLS), lambda i, j: (i, 0))],
        out_shape=[jax.ShapeDtypeStruct((t, n_main_cols), BF16),
                   jax.ShapeDtypeStruct((t, SMALL_COLS), F32)],
        scratch_shapes=[pltpu.VMEM((tm, d), BF16)],
        compiler_params=_cparams("arbitrary", "arbitrary"),
        name="in_proj",
    )(x, g, sc_a, sh_a, w_packed)


def _mlstm_kernel(bg_ref, q_ref, k_ref, v_ref, og_ref, zs_ref, gml_ref, c0_ref, n0_ref, m0_ref,
                  h_ref, c_ref, n_ref, m_ref, *, length, rows, scale):
    head = pl.program_id(1)
    chunk = pl.program_id(2)

    @pl.when(chunk == 0)
    def _():
        c_ref[...] = c0_ref[...]
        n_ref[...] = n0_ref[...]
        m_ref[...] = m0_ref[...]

    def pad(a):
        if rows == length:
            return a
        return jnp.concatenate([a, jnp.zeros((length - rows, a.shape[1]), a.dtype)], axis=0)

    q = pad(q_ref[...])
    k = pad(k_ref[...])
    v = pad(v_ref[...])
    zs = pad(zs_ref[...])

    srow = lax.broadcasted_iota(jnp.int32, (16, SMALL_COLS), 0)
    scol = lax.broadcasted_iota(jnp.int32, (16, SMALL_COLS), 1)
    want = jnp.where(srow < 8, GATE_COL + head, GATE_COL + ML_HEADS + head)
    sel = jnp.where(scol == want, 1.0, 0.0).astype(BF16)
    gates = sum(_dot_nt(sel, part) for part in _split3(zs))
    lane = lax.broadcasted_iota(jnp.int32, (1, length), 1)
    valid = lane < rows
    ig_row = jnp.where(valid, gates[0:1] + bg_ref[0, head], NEG)
    fpre = gates[8:9] + bg_ref[1, head]
    lf_row = jnp.where(valid, jnp.minimum(fpre, 0.0) - jnp.log(1.0 + jnp.exp(-jnp.abs(fpre))), 0.0)

    r_i = lax.broadcasted_iota(jnp.int32, (length, length), 0)
    c_i = lax.broadcasted_iota(jnp.int32, (length, length), 1)
    upper = jnp.where(r_i <= c_i, 1.0, 0.0).astype(BF16)
    lf8 = jnp.broadcast_to(lf_row, (8, length))
    bc_row = sum(_dot(part, upper) for part in _split3(lf8))[0:1]
    eye = jnp.where(r_i == c_i, 1.0, 0.0).astype(BF16)
    stacked = jnp.concatenate([jnp.broadcast_to(bc_row, (LANES // 2, length)),
                               jnp.broadcast_to(ig_row, (LANES // 2, length))], axis=0)
    cols = sum(_dot_nt(eye, part) for part in _split3(stacked))
    bc_col = cols[:, 0:1]
    ig_col = cols[:, LANES // 2:LANES // 2 + 1]

    m_prev = m_ref[...]
    dmat = jnp.where(r_i >= c_i, bc_col - bc_row + ig_row, NEG)
    a_col = bc_col + m_prev
    m_t = jnp.maximum(a_col, jnp.max(dmat, axis=1, keepdims=True))
    w_intra = jnp.exp(dmat - m_t)
    w_inter = jnp.exp(a_col - m_t)
    sc = _dot_nt(q, k) * scale * w_intra
    c_state = c_ref[...]
    n_state = n_ref[...]
    num = _dot(sc.astype(BF16), v) + w_inter * _dot(q, c_state.astype(BF16))
    qn = jnp.sum(q.astype(F32) * n_state, axis=1, keepdims=True)
    nq = jnp.sum(sc, axis=1, keepdims=True) + w_inter * qn
    hh = num / jnp.maximum(jnp.abs(nq), jnp.exp(-m_t))

    hv = hh[:rows]
    hn = _rms(hv, gml_ref[...])
    h_ref[...] = (hn * jax.nn.sigmoid(og_ref[...].astype(F32))).astype(BF16)

    m_new = m_t[length - 1:length, :]
    bc_last = bc_row[:, length - 1:length]
    g_state = jnp.exp(bc_last + m_prev - m_new)
    g_tok = jnp.exp(bc_last - bc_col + ig_col - m_new)
    kg = k.astype(F32) * (scale * g_tok)
    c_ref[...] = g_state * c_state + _dot_tn(kg.astype(BF16), v)
    n_ref[...] = g_state * n_state + jnp.sum(kg, axis=0, keepdims=True)
    m_ref[...] = m_new


def _mlstm(zm, zs, bg, g_mlnorm, c0, n0, m0, batch, seq, rows, length, dqk, dv):
    t = zm.shape[0]
    nc = seq // rows
    hh = ML_HEADS
    kernel = functools.partial(_mlstm_kernel, length=length, rows=rows, scale=dqk ** -0.5)
    row = lambda b, h, c: b * nc + c
    st4 = lambda b, h, c: (b, h, 0, 0)
    outs = pl.pallas_call(
        kernel,
        grid=(batch, hh, nc),
        in_specs=[pl.BlockSpec(memory_space=pltpu.SMEM),
                  pl.BlockSpec((rows, dqk), lambda b, h, c: (row(b, h, c), h)),
                  pl.BlockSpec((rows, dqk), lambda b, h, c: (row(b, h, c), hh + h)),
                  pl.BlockSpec((rows, dv), lambda b, h, c: (row(b, h, c), hh + h)),
                  pl.BlockSpec((rows, dv), lambda b, h, c: (row(b, h, c), 2 * hh + h)),
                  pl.BlockSpec((rows, SMALL_COLS), lambda b, h, c: (row(b, h, c), 0)),
                  pl.BlockSpec((1, dv), lambda b, h, c: (0, h)),
                  pl.BlockSpec((None, None, dqk, dv), st4),
                  pl.BlockSpec((None, None, 1, dqk), st4),
                  pl.BlockSpec((None, None, 1, 1), st4)],
        out_specs=[pl.BlockSpec((rows, dv), lambda b, h, c: (row(b, h, c), h)),
                   pl.BlockSpec((None, None, dqk, dv), st4),
                   pl.BlockSpec((None, None, 1, dqk), st4),
                   pl.BlockSpec((None, None, 1, 1), st4)],
        out_shape=[jax.ShapeDtypeStruct((t, hh * dv), BF16),
                   jax.ShapeDtypeStruct((batch, hh, dqk, dv), F32),
                   jax.ShapeDtypeStruct((batch, hh, 1, dqk), F32),
                   jax.ShapeDtypeStruct((batch, hh, 1, 1), F32)],
        compiler_params=_cparams("arbitrary", "arbitrary", "arbitrary"),
        name="mlstm",
    )(bg, zm, zm, zm, zm, zs, g_mlnorm, c0, n0[:, :, None, :], m0[:, :, None, None])
    h, c, n, m = outs
    return h, c, n[:, :, 0, :], m[:, :, 0, 0]


def _mla_prep_kernel(cq_ref, ckv_ref, zs_ref, pos_ref, invf_ref, gcq_ref, gckv_ref,
                     wuq_ref, wuk_ref, wuvt_ref,
                     q_ref, k_ref, vt_ref, ckv_out_ref, kr_out_ref):
    tm = cq_ref.shape[0]
    ang = pos_ref[...] * invf_ref[...]
    lane = lax.broadcasted_iota(jnp.int32, (tm, LANES), 1)
    half = MLA_ROPE // 2
    cosv = jnp.cos(ang)
    sinv = jnp.sin(ang)
    c_t = jnp.where(lane < MLA_ROPE, cosv, 0.0)
    s_lo = jnp.where(lane < half, -sinv, 0.0)
    s_hi = jnp.where((lane >= half) & (lane < MLA_ROPE), sinv, 0.0)

    def rope(x):
        return x * c_t + pltpu.roll(x, LANES - half, 1) * s_lo + pltpu.roll(x, half, 1) * s_hi

    cqn = _rms(cq_ref[...].astype(F32), gcq_ref[...]).astype(BF16)
    for h in range(MLA_HEADS):
        qh = _dot(cqn, wuq_ref[:, h * HEAD_PAD:(h + 1) * HEAD_PAD])
        q_ref[h, :, 0:LANES] = (qh[:, 0:LANES] * Q_SCALE).astype(BF16)
        q_ref[h, :, LANES:HEAD_PAD] = (rope(qh[:, LANES:HEAD_PAD]) * Q_SCALE).astype(BF16)

    ckvn = _rms(ckv_ref[...].astype(F32), gckv_ref[...])
    ckv_out_ref[...] = ckvn
    cb = ckvn.astype(BF16)
    kr = rope(zs_ref[:, 0:LANES])
    kr_out_ref[...] = kr[:, 0:MLA_ROPE]
    krb = kr.astype(BF16)
    kn = _dot(cb, wuk_ref[...])
    for h in range(MLA_HEADS):
        k_ref[h, :, 0:LANES] = kn[:, h * MLA_NOPE:(h + 1) * MLA_NOPE].astype(BF16)
        k_ref[h, :, LANES:HEAD_PAD] = krb
    vt = _dot_nt(wuvt_ref[...], cb)
    vt_ref[...] = vt.reshape(MLA_HEADS, MLA_DV, tm).astype(BF16)


def _mla_prep(zm, zs, pos, invf, g_cq, g_ckv, wuq, wuk, wuvt, cq_off, ckv_off, tm):
    t = zm.shape[0]
    ql = g_cq.shape[-1]
    kvl = g_ckv.shape[-1]
    hh = MLA_HEADS
    return pl.pallas_call(
        _mla_prep_kernel,
        grid=(t // tm,),
        in_specs=[pl.BlockSpec((tm, ql), lambda i: (i, cq_off // ql)),
                  pl.BlockSpec((tm, kvl), lambda i: (i, ckv_off // kvl)),
                  pl.BlockSpec((tm, SMALL_COLS), lambda i: (i, 0)),
                  pl.BlockSpec((tm, 1), lambda i: (i, 0)),
                  _const_spec((1, LANES)),
                  _const_spec((1, ql)),
                  _const_spec((1, kvl)),
                  _const_spec(wuq.shape),
                  _const_spec(wuk.shape),
                  _const_spec(wuvt.shape)],
        out_specs=[pl.BlockSpec((hh, tm, HEAD_PAD), lambda i: (0, i, 0)),
                   pl.BlockSpec((hh, tm, HEAD_PAD), lambda i: (0, i, 0)),
                   pl.BlockSpec((hh, None, MLA_DV, tm), lambda i: (0, i, 0, 0)),
                   pl.BlockSpec((tm, kvl), lambda i: (i, 0)),
                   pl.BlockSpec((tm, MLA_ROPE), lambda i: (i, 0))],
        out_shape=[jax.ShapeDtypeStruct((hh, t, HEAD_PAD), BF16),
                   jax.ShapeDtypeStruct((hh, t, HEAD_PAD), BF16),
                   jax.ShapeDtypeStruct((hh, t // tm, MLA_DV, tm), BF16),
                   jax.ShapeDtypeStruct((t, kvl), F32),
                   jax.ShapeDtypeStruct((t, MLA_ROPE), F32)],
        compiler_params=_cparams("arbitrary"),
        name="mla_prep",
    )(zm, zm, zs, pos, invf, g_cq, g_ckv, wuq, wuk, wuvt)


def _attn_kernel(q_ref, k_ref, vt_ref, o_ref, acc_ref, m_ref, l_ref, *, seq, tq):
    nq = seq // tq
    tv = vt_ref.shape[-1]
    sub = tq // tv

    def q_body(qi, carry):
        q0 = pl.multiple_of(qi * tq, tq)
        q = q_ref[pl.ds(q0, tq), :]
        m_ref[...] = jnp.full(m_ref.shape, NEG, F32)
        l_ref[...] = jnp.zeros(l_ref.shape, F32)
        acc_ref[...] = jnp.zeros(acc_ref.shape, F32)

        def kv_step(ki, masked):
            k0 = pl.multiple_of(ki * tq, tq)
            kb = k_ref[pl.ds(k0, tq), :]
            s = _dot_nt(kb, q)
            if masked:
                kc = lax.broadcasted_iota(jnp.int32, s.shape, 0) // CHUNK
                qc = lax.broadcasted_iota(jnp.int32, s.shape, 1) // CHUNK
                s = jnp.where(kc <= qc, s, NEG)
            m_prev = m_ref[...]
            m_new = jnp.maximum(m_prev, jnp.max(s, axis=0, keepdims=True))
            alpha = jnp.exp2(m_prev - m_new)
            p = jnp.exp2(s - m_new)
            l_ref[...] = alpha * l_ref[...] + jnp.sum(p, axis=0, keepdims=True)
            pb = p.astype(BF16)
            pv = _dot(vt_ref[ki * sub], pb[0:tv])
            for j in range(1, sub):
                pv = pv + _dot(vt_ref[ki * sub + j], pb[j * tv:(j + 1) * tv])
            acc_ref[...] = alpha * acc_ref[...] + pv
            m_ref[...] = m_new

        def pair_body(pj, c):
            kv_step(2 * pj, False)
            kv_step(2 * pj + 1, False)
            return c

        lax.fori_loop(0, qi // 2, pair_body, 0)

        @pl.when(qi % 2 == 1)
        def _():
            kv_step(qi - 1, False)

        kv_step(qi, True)
        out = acc_ref[...] * (1.0 / l_ref[...])
        o_ref[pl.ds(q0, tq), :] = out.T.astype(BF16)
        return carry

    lax.fori_loop(0, nq, q_body, 0)


def _attention(q, k, vt, batch, seq, tq):
    hh = MLA_HEADS
    tv = vt.shape[-1]
    nblk = seq // tv
    return pl.pallas_call(
        functools.partial(_attn_kernel, seq=seq, tq=tq),
        grid=(batch, hh),
        in_specs=[pl.BlockSpec((None, seq, HEAD_PAD), lambda b, h: (h, b, 0)),
                  pl.BlockSpec((None, seq, HEAD_PAD), lambda b, h: (h, b, 0)),
                  pl.BlockSpec((None, nblk, MLA_DV, tv), lambda b, h: (h, b, 0, 0))],
        out_specs=pl.BlockSpec((seq, MLA_DV), lambda b, h: (b, h)),
        out_shape=jax.ShapeDtypeStruct((batch * seq, hh * MLA_DV), BF16),
        scratch_shapes=[pltpu.VMEM((MLA_DV, tq), F32),
                        pltpu.VMEM((1, tq), F32),
                        pltpu.VMEM((1, tq), F32)],
        compiler_params=_cparams("arbitrary", "arbitrary"),
        name="mla_attention",
    )(q, k, vt)


def _absorb_kernel(q_ref, wukt_ref, o_ref, *, kvl):
    q = q_ref[...]
    ql = _dot(q[:, 0:MLA_NOPE], wukt_ref[...])
    full = jnp.concatenate([ql.astype(BF16), q[:, LANES:HEAD_PAD]], axis=1)
    o_ref[...] = full.reshape(o_ref.shape)


def _sample_attn_kernel(q_ref, cpast_ref, kpast_ref, cnew_ref, knew_ref, o_ref, *, kvl, new):
    qb = q_ref[...]
    ql = qb[:, 0:kvl]
    qr = qb[:, kvl:kvl + LANES]
    c_past = cpast_ref[...].astype(BF16)
    k_past = kpast_ref[...].astype(BF16)
    s_past = _dot_nt(ql, c_past) + _dot_nt(qr, k_past)
    fill = LANES - new
    c_new = jnp.concatenate([cnew_ref[...].astype(BF16), jnp.zeros((fill, kvl), BF16)], axis=0)
    k_new = jnp.concatenate([knew_ref[...], jnp.zeros((fill, LANES), BF16)], axis=0)
    s_new = _dot_nt(ql, c_new) + _dot_nt(qr, k_new)
    lane = lax.broadcasted_iota(jnp.int32, s_new.shape, 1)
    s_new = jnp.where(lane < new, s_new, NEG)
    m = jnp.maximum(jnp.max(s_past, axis=1, keepdims=True), jnp.max(s_new, axis=1, keepdims=True))
    p_past = jnp.exp2(s_past - m)
    p_new = jnp.exp2(s_new - m)
    l = jnp.sum(p_past, axis=1, keepdims=True) + jnp.sum(p_new, axis=1, keepdims=True)
    lat = _dot(p_past.astype(BF16), c_past) + _dot(p_new.astype(BF16), c_new)
    o_ref[...] = (lat / l).astype(BF16)


def _uv_kernel(lat_ref, wuv_ref, o_ref):
    lat = lat_ref[...]
    lat2 = lat.reshape(lat.shape[0] * lat.shape[1], lat.shape[2])
    o_ref[...] = _dot(lat2, wuv_ref[...]).astype(BF16)


def _sample_mla(q_s, k_s, ckv_new, cache_ckv, cache_krope_pad, wukt, wuv_h, batch, new):
    hh = MLA_HEADS
    t = batch * new
    kvl = ckv_new.shape[-1]
    width = kvl + LANES
    past = cache_ckv.shape[1]
    q_abs = pl.pallas_call(
        functools.partial(_absorb_kernel, kvl=kvl),
        grid=(hh,),
        in_specs=[pl.BlockSpec((None, t, HEAD_PAD), lambda h: (h, 0, 0)),
                  pl.BlockSpec((None, MLA_NOPE, kvl), lambda h: (h, 0, 0))],
        out_specs=pl.BlockSpec((batch, new, width), lambda h: (0, h, 0)),
        out_shape=jax.ShapeDtypeStruct((batch, hh * new, width), BF16),
        compiler_params=_cparams("arbitrary"),
        name="mla_absorb",
    )(q_s, wukt)
    lat = pl.pallas_call(
        functools.partial(_sample_attn_kernel, kvl=kvl, new=new),
        grid=(batch,),
        in_specs=[pl.BlockSpec((None, hh * new, width), lambda b: (b, 0, 0)),
                  pl.BlockSpec((None, past, kvl), lambda b: (b, 0, 0)),
                  pl.BlockSpec((None, past, LANES), lambda b: (b, 0, 0)),
                  pl.BlockSpec((new, kvl), lambda b: (b, 0)),
                  pl.BlockSpec((None, new, LANES), lambda b: (0, b, 1))],
        out_specs=pl.BlockSpec((None, hh * new, kvl), lambda b: (b, 0, 0)),
        out_shape=jax.ShapeDtypeStruct((batch, hh * new, kvl), BF16),
        compiler_params=_cparams("arbitrary"),
        name="mla_sample_attention",
    )(q_abs, cache_ckv, cache_krope_pad, ckv_new, k_s)
    return pl.pallas_call(
        _uv_kernel,
        grid=(hh,),
        in_specs=[pl.BlockSpec((batch, new, kvl), lambda h: (0, h, 0)),
                  pl.BlockSpec((None, kvl, MLA_DV), lambda h: (h, 0, 0))],
        out_specs=pl.BlockSpec((t, MLA_DV), lambda h: (0, h)),
        out_shape=jax.ShapeDtypeStruct((t, hh * MLA_DV), BF16),
        compiler_params=_cparams("arbitrary"),
        name="mla_uv",
    )(lat, wuv_h)


def _outproj_kernel(hm_ref, om_ref, w_ref, x_ref, gt_ref, g_ref, gpre2_ref, sc2_ref, sh2_ref,
                    o_ref, u_ref, *, half, rpg):
    kk = pl.program_id(1)
    a = jnp.where(kk < half, hm_ref[...], om_ref[...])
    _accumulate_cols(o_ref, a, w_ref, kk == 0)

    @pl.when(kk == 2 * half - 1)
    def _():
        def body(r0, strip):
            rows = pl.ds(r0, strip)
            y = _rms(o_ref[rows, :], g_ref[...])
            x1 = x_ref[rows, :] + _expand_mod(gt_ref, r0, strip, rpg) * y
            o_ref[rows, :] = x1
            u = (_rms(x1, gpre2_ref[...]) * (1.0 + _expand_mod(sc2_ref, r0, strip, rpg))
                 + _expand_mod(sh2_ref, r0, strip, rpg))
            u_ref[rows, :] = u.astype(BF16)

        _for_strips(x_ref.shape[0], body)


def _outproj(hml, omla, w_out, x, gt, g, gpre2, sc2, sh2, rows_per_group, tm, tk):
    t, d = x.shape
    half = hml.shape[1] // tk
    gt_a, gt_s = _mod_operand(gt, rows_per_group, tm)
    sc_a, sc_s = _mod_operand(sc2, rows_per_group, tm)
    sh_a, sh_s = _mod_operand(sh2, rows_per_group, tm)
    vec = pl.BlockSpec((1, d), lambda i, k: (0, 0))
    return pl.pallas_call(
        functools.partial(_outproj_kernel, half=half, rpg=rows_per_group),
        grid=(t // tm, 2 * half),
        in_specs=[pl.BlockSpec((tm, tk), lambda i, k: (i, jnp.minimum(k, half - 1))),
                  pl.BlockSpec((tm, tk), lambda i, k: (i, jnp.maximum(k - half, 0))),
                  pl.BlockSpec((tk, d), lambda i, k: (k, 0)),
                  pl.BlockSpec((tm, d), lambda i, k: (i, 0)),
                  gt_s, vec, vec, sc_s, sh_s],
        out_specs=[pl.BlockSpec((tm, d), lambda i, k: (i, 0)),
                   pl.BlockSpec((tm, d), lambda i, k: (i, 0))],
        out_shape=[jax.ShapeDtypeStruct((t, d), F32),
                   jax.ShapeDtypeStruct((t, d), BF16)],
        compiler_params=_cparams("arbitrary", "arbitrary"),
        name="out_proj",
    )(hml, omla, w_out, x, gt_a, g, gpre2, sc_a, sh_a)


def _ffn_kernel(u_ref, gt_ref, gpost_ref, w1_ref, w2_ref, x_hbm, o_hbm,
                acc_ref, xs_ref, sem_in, sem_out, *, rpg):
    i = pl.program_id(0)
    f = pl.program_id(1)
    tm = acc_ref.shape[0]
    strip = xs_ref.shape[1]

    h = jnp.maximum(_dot(u_ref[...], w1_ref[...]), 0.0)
    _accumulate_cols(acc_ref, (h * h).astype(BF16), w2_ref, f == 0)

    @pl.when(f == pl.num_programs(1) - 1)
    def _():
        n = tm // strip
        row0 = i * tm

        def x_copy(s, slot):
            return pltpu.make_async_copy(x_hbm.at[pl.ds(row0 + s * strip, strip), :],
                                         xs_ref.at[slot], sem_in.at[slot])

        def o_copy(s):
            return pltpu.make_async_copy(acc_ref.at[pl.ds(s * strip, strip), :],
                                         o_hbm.at[pl.ds(row0 + s * strip, strip), :],
                                         sem_out.at[s])

        x_copy(0, 0).start()
        for s in range(n):
            slot = s % 2
            if s + 1 < n:
                x_copy(s + 1, 1 - slot).start()
            x_copy(s, slot).wait()
            rows = pl.ds(s * strip, strip)
            y = _rms(acc_ref[rows, :], gpost_ref[...])
            acc_ref[rows, :] = xs_ref[slot] + _expand_mod(gt_ref, s * strip, strip, rpg) * y
            o_copy(s).start()
        for s in range(n):
            o_copy(s).wait()


def _ffn(x, u, gt, gpost, w1, w2, rows_per_group, tm, tf):
    t, d = x.shape
    ff = w1.shape[1]
    strip = min(STRIP_ROWS, tm)
    gt_a, gt_s = _mod_operand(gt, rows_per_group, tm)
    return pl.pallas_call(
        functools.partial(_ffn_kernel, rpg=rows_per_group),
        grid=(t // tm, ff // tf),
        in_specs=[pl.BlockSpec((tm, d), lambda i, f: (i, 0), pipeline_mode=pl.Buffered(1)),
                  gt_s,
                  pl.BlockSpec((1, d), lambda i, f: (0, 0)),
                  pl.BlockSpec((d, tf), lambda i, f: (0, f)),
                  pl.BlockSpec((tf, d), lambda i, f: (f, 0)),
                  pl.BlockSpec(memory_space=pl.ANY)],
        out_specs=pl.BlockSpec(memory_space=pl.ANY),
        out_shape=jax.ShapeDtypeStruct((t, d), F32),
        scratch_shapes=[pltpu.VMEM((tm, d), F32),
                        pltpu.VMEM((2, strip, d), F32),
                        pltpu.SemaphoreType.DMA((2,)),
                        pltpu.SemaphoreType.DMA((tm // strip,))],
        compiler_params=_cparams("arbitrary", "arbitrary"),
        name="ffn",
    )(u, gt_a, gpost, w1, w2, x)


def _pack_weights(w_in, w_uq, w_uk, w_uv, w_out, w_ff1, w_ff2, d):
    ml_width = d // 2
    ml_dv = ml_width // ML_HEADS
    ml_dqk = ml_dv // 2
    ml_qk = ML_HEADS * ml_dqk
    ql = w_uq.shape[0]
    kvl = w_uk.shape[0]
    o_q = 2 * ml_qk + 2 * ml_width
    o_ig, o_fg = o_q, o_q + ML_HEADS
    o_cq = o_fg + ML_HEADS
    o_ckv = o_cq + ql
    o_kr = o_ckv + kvl
    n_main = o_q + ql + kvl
    tn = 512 if n_main % 512 == 0 else 256
    zeros = lambda n: jnp.zeros((d, n), w_in.dtype)
    w_packed = jnp.concatenate(
        [w_in[:, :o_q], w_in[:, o_cq:o_ckv], w_in[:, o_ckv:o_kr],
         w_in[:, o_kr:o_kr + MLA_ROPE], zeros(GATE_COL - MLA_ROPE),
         w_in[:, o_ig:o_ig + ML_HEADS], w_in[:, o_fg:o_fg + ML_HEADS],
         zeros(tn - GATE_COL - 2 * ML_HEADS)], axis=1).astype(BF16)
    wuq = jnp.concatenate([w_uq, jnp.zeros((ql, MLA_HEADS, HEAD_PAD - MLA_NOPE - MLA_ROPE), w_uq.dtype)],
                          axis=-1).reshape(ql, MLA_HEADS * HEAD_PAD).astype(BF16)
    wuk = w_uk.reshape(kvl, MLA_HEADS * MLA_NOPE).astype(BF16)
    wukt = jnp.transpose(w_uk, (1, 2, 0)).astype(BF16)
    wuvt = jnp.transpose(w_uv, (1, 2, 0)).reshape(MLA_HEADS * MLA_DV, kvl).astype(BF16)
    wuv_h = jnp.transpose(w_uv, (1, 0, 2)).astype(BF16)
    dims = dict(ml_width=ml_width, ml_dv=ml_dv, ml_dqk=ml_dqk, n_main=n_main, tn=tn,
                cq_off=o_q, ckv_off=o_q + ql)
    return dict(w_in=w_packed, wuq=wuq, wuk=wuk, wukt=wukt, wuvt=wuvt, wuv_h=wuv_h,
                w_out=w_out.astype(BF16), w_ff1=w_ff1.astype(BF16), w_ff2=w_ff2.astype(BF16)), dims


def _group(x2, mods, pos, rows_per_group, batch, seq, ml_rows, ml_len, state,
           wts, dims, vecs, invf, tm):
    sh1, sc1 = mods[0], mods[1]
    zm, zs = _inproj(x2, vecs["g_pre1"], sc1, sh1, wts["w_in"], dims["n_main"], dims["tn"],
                     rows_per_group, tm)
    hml, c_new, n_new, m_new = _mlstm(zm, zs, vecs["bg"], vecs["g_mlnorm"], *state, batch, seq,
                                      ml_rows, ml_len, dims["ml_dqk"], dims["ml_dv"])
    q, k, vt, ckv, kr = _mla_prep(zm, zs, pos, invf, vecs["g_cq"], vecs["g_ckv"], wts["wuq"],
                                  wts["wuk"], wts["wuvt"], dims["cq_off"], dims["ckv_off"], tm)
    return hml, (c_new, n_new, m_new), q, k, vt, ckv, kr


def _finish(x2, hml, omla, mods, rows_per_group, wts, vecs, tm):
    _, _, gt1, sh2, sc2, gt2 = mods
    d = x2.shape[1]
    tk = min(512, hml.shape[1])
    x1, u2 = _outproj(hml, omla, wts["w_out"], x2, gt1, vecs["g_post1"], vecs["g_pre2"], sc2, sh2,
                      rows_per_group, tm, tk)
    tf = min(1024, wts["w_ff1"].shape[1])
    return _ffn(x1, u2, gt2, vecs["g_post2"], wts["w_ff1"], wts["w_ff2"], rows_per_group, tm, tf)


def kernel(x_prompt, x_sample, cache_mla_ckv, cache_mla_krope, state_mlstm_C, state_mlstm_n, state_mlstm_m, c_prompt, c_sample, w_ada, b_ada, g_pre1, g_post1, w_in, b_ig, b_fg, g_mlnorm, g_cq, w_uq, g_ckv, w_uk, w_uv, w_out, g_pre2, g_post2, w_ff1, w_ff2):
    bp, sp, d = x_prompt.shape
    bs, ss, _ = x_sample.shape
    depth = w_in.shape[0]
    past = cache_mla_ckv.shape[2]
    half = MLA_ROPE // 2
    inv = ROPE_THETA ** (-jnp.arange(half, dtype=F32) / half)
    invf = jnp.concatenate([inv, inv, jnp.zeros((LANES - MLA_ROPE,), F32)])[None, :]
    pos_p = jnp.tile(jnp.arange(sp, dtype=F32), bp)[:, None]
    pos_s = jnp.tile(past + jnp.arange(ss, dtype=F32), bs)[:, None]
    tp, ts = bp * sp, bs * ss
    tm_p = min(512, sp)
    tm_s = min(512, ts)
    ml_rows_p = min(256, sp)
    ml_len_s = max(LANES, ss)

    xp = x_prompt.reshape(tp, d)
    xs = x_sample.reshape(ts, d)
    rows_c = bp + bs
    c_all = jnp.concatenate([c_prompt, c_sample, jnp.zeros((-rows_c % 8, d), F32)], axis=0)
    outs = [[] for _ in range(10)]
    for l in range(depth):
        wts, dims = _pack_weights(w_in[l], w_uq[l], w_uk[l], w_uv[l], w_out[l], w_ff1[l], w_ff2[l], d)
        vecs = dict(g_pre1=g_pre1[l][None], g_post1=g_post1[l][None], g_pre2=g_pre2[l][None],
                    g_post2=g_post2[l][None], g_mlnorm=g_mlnorm[l][None], g_cq=g_cq[l][None],
                    g_ckv=g_ckv[l][None], bg=jnp.stack([b_ig[l], b_fg[l]]))
        mod = _modulation(c_all, w_ada[l], b_ada[l][None])
        mods_p = [mod[:bp, i * d:(i + 1) * d] for i in range(N_MOD)]
        mods_s = [mod[bp:rows_c, i * d:(i + 1) * d] for i in range(N_MOD)]

        dqk, dv = dims["ml_dqk"], dims["ml_dv"]
        zero_state = (jnp.zeros((bp, ML_HEADS, dqk, dv), F32), jnp.zeros((bp, ML_HEADS, dqk), F32),
                      jnp.zeros((bp, ML_HEADS), F32))
        hml_p, st_p, q_p, k_p, vt_p, ckv_p, kr_p = _group(
            xp, mods_p, pos_p, sp, bp, sp, ml_rows_p, ml_rows_p, zero_state,
            wts, dims, vecs, invf, tm_p)
        omla_p = _attention(q_p, k_p, vt_p, bp, sp, min(1024, sp))
        xp = _finish(xp, hml_p, omla_p, mods_p, sp, wts, vecs, tm_p)

        state_s = (state_mlstm_C[l], state_mlstm_n[l], state_mlstm_m[l])
        hml_s, st_s, q_s, k_s, _, ckv_s, kr_s = _group(
            xs, mods_s, pos_s, ss, bs, ss, ss, ml_len_s, state_s,
            wts, dims, vecs, invf, tm_s)
        krope_pad = jnp.pad(cache_mla_krope[l], ((0, 0), (0, 0), (0, LANES - MLA_ROPE)))
        omla_s = _sample_mla(q_s, k_s, ckv_s, cache_mla_ckv[l], krope_pad, wts["wukt"],
                             wts["wuv_h"], bs, ss)
        xs = _finish(xs, hml_s, omla_s, mods_s, ss, wts, vecs, tm_s)

        kvl = ckv_p.shape[-1]
        for lst, val in zip(outs, (ckv_p.reshape(bp, sp, kvl), kr_p.reshape(bp, sp, MLA_ROPE), *st_p,
                                   ckv_s.reshape(bs, ss, kvl), kr_s.reshape(bs, ss, MLA_ROPE), *st_s)):
            lst.append(val)
    return (xp.reshape(bp, sp, d), xs.reshape(bs, ss, d), *[jnp.stack(o) for o in outs])
```
